```python
import math
import jax, jax.numpy as jnp
from jax import lax
import numpy as np

D_MODEL = 1024
BATCH = 2
SEQ = 8192
DEPTH = 1
DEC_BATCH = 8
DEC_SEQ = 4096
PAST_LEN = 128

HEAD_DIM = 64
SSD_HEADS = 8
SSD_WIDTH = SSD_HEADS * HEAD_DIM
SSD_GROUPS = 2
SSD_STATE = 128
SSD_CONV = 5
SSD_CHUNK = 128
ATT_HEADS = 8
ATT_WIDTH = ATT_HEADS * HEAD_DIM
MIX_WIDTH = SSD_WIDTH + ATT_WIDTH
DILATED_PATTERNS = ((128, 1), (512, 4), (2048, 16))
ROPE_DIMS = HEAD_DIM // 4
ROPE_THETA = 500000.0
MEM_LEN = 256
XATT_HEADS = 4
XATT_HEAD_DIM = D_MODEL // XATT_HEADS
D_FF = 4 * D_MODEL
EPS = 1e-6
NEG_INF = -1e30
CONV_CH = SSD_WIDTH + 2 * SSD_GROUPS * SSD_STATE
OFF_Z = 0
OFF_XBC = OFF_Z + SSD_WIDTH
OFF_DT = OFF_XBC + CONV_CH
OFF_Q = OFF_DT + 2 * SSD_HEADS
OFF_K = OFF_Q + ATT_WIDTH
OFF_V = OFF_K + ATT_WIDTH
IN_COLS = OFF_V + ATT_WIDTH

kernel_name = 'hymba_ssd_dilated_encoder'


def rmsnorm(x, g):
    xf = x.astype(jnp.float32)
    y = xf * lax.rsqrt(jnp.mean(xf * xf, axis=-1, keepdims=True) + EPS)
    return (y * g.astype(jnp.float32)).astype(x.dtype)


def rope_partial(t, pos):
    half = ROPE_DIMS // 2
    inv_freq = jnp.power(jnp.float32(ROPE_THETA), -jnp.arange(half, dtype=jnp.float32) / half)
    ang = pos.astype(jnp.float32)[:, None] * inv_freq[None, :]
    cos = jnp.cos(ang)[:, None, :]
    sin = jnp.sin(ang)[:, None, :]
    t1 = t[..., :half]
    t2 = t[..., half:ROPE_DIMS]
    return jnp.concatenate([t1 * cos - t2 * sin, t2 * cos + t1 * sin, t[..., ROPE_DIMS:]], axis=-1)


def centred_depthwise_conv(u, w, b):
    pad = (SSD_CONV - 1) // 2
    out = lax.conv_general_dilated(u, w[:, None, :].astype(u.dtype), window_strides=(1,),
                                   padding=[(pad, pad)], dimension_numbers=('NWC', 'WIO', 'NWC'),
                                   feature_group_count=u.shape[-1])
    return out + b.astype(u.dtype)


def ssd_chunked_scan(x, dt, A, Bm, Cm):
    bsz, l, h, p = x.shape
    g, n = Bm.shape[2], Bm.shape[3]
    r = h // g
    q = SSD_CHUNK
    c = l // q
    x = x.reshape(bsz, c, q, g, r, p)
    dt = dt.reshape(bsz, c, q, g, r)
    Bm = Bm.reshape(bsz, c, q, g, n)
    Cm = Cm.reshape(bsz, c, q, g, n)
    a_cs = jnp.cumsum(dt * A.reshape(g, r), axis=2)
    xdt = x * dt[..., None]
    causal = jnp.tril(jnp.ones((q, q), dtype=bool))[:, :, None, None]
    seg = a_cs[:, :, :, None] - a_cs[:, :, None, :]
    lmat = jnp.where(causal, jnp.exp(jnp.where(causal, seg, 0.0)), 0.0)
    cb = jnp.einsum('bcign,bcjgn->bcijg', Cm, Bm)
    y_diag = jnp.einsum('bcijg,bcijgr,bcjgrp->bcigrp', cb, lmat, xdt)
    decay_to_end = jnp.exp(a_cs[:, :, -1:] - a_cs)
    chunk_states = jnp.einsum('bcjgn,bcjgr,bcjgrp->bcgrpn', Bm, decay_to_end, xdt)
    chunk_decay = jnp.exp(a_cs[:, :, -1])

    def step(carry, inp):
        st, dec = inp
        return carry * dec[..., None, None] + st, carry

    init = jnp.zeros((bsz, g, r, p, n), jnp.float32)
    _, prev = lax.scan(step, init, (jnp.moveaxis(chunk_states, 1, 0), jnp.moveaxis(chunk_decay, 1, 0)))
    prev = jnp.moveaxis(prev, 0, 1)
    y_off = jnp.einsum('bcign,bcgrpn,bcigr->bcigrp', Cm, prev, jnp.exp(a_cs))
    return (y_diag + y_off).reshape(bsz, l, h, p)


def ssd_mixer(u_z, u_xbc, u_dt, conv_w, conv_b, A_log, dt_bias, D, norm_g):
    bsz, s, _ = u_z.shape
    xbc = jax.nn.silu(centred_depthwise_conv(u_xbc, conv_w, conv_b)).astype(jnp.float32)
    xs = xbc[..., :SSD_WIDTH].reshape(bsz, s, SSD_HEADS, HEAD_DIM)
    nbc = SSD_GROUPS * SSD_STATE
    Bm = xbc[..., SSD_WIDTH:SSD_WIDTH + nbc].reshape(bsz, s, SSD_GROUPS, SSD_STATE)
    Cm = xbc[..., SSD_WIDTH + nbc:].reshape(bsz, s, SSD_GROUPS, SSD_STATE)
    dt = jax.nn.softplus(u_dt.astype(jnp.float32).reshape(bsz, s, 2, SSD_HEADS) + dt_bias.astype(jnp.float32))
    A = -jnp.exp(A_log.astype(jnp.float32))
    flip = lambda t: jnp.flip(t, axis=1)
    y_fwd = ssd_chunked_scan(xs, dt[:, :, 0], A[0], Bm, Cm)
    y_bwd = flip(ssd_chunked_scan(flip(xs), flip(dt[:, :, 1]), A[1], flip(Bm), flip(Cm)))
    y = (y_fwd + y_bwd + D.astype(jnp.float32)[:, None] * xs).reshape(bsz, s, SSD_WIDTH)
    y = y * jax.nn.silu(u_z.astype(jnp.float32))
    return rmsnorm(y, norm_g)


def band_pattern(q, k, v, dil, half):
    bsz, s, h, e = q.shape
    L = s // dil
    blk = half
    nb = -(-L // blk)
    Lp = nb * blk

    def to_classes(t, lo, hi):
        t = t.reshape(bsz, L, dil, h, e).transpose(0, 2, 1, 3, 4)
        return jnp.pad(t, ((0, 0), (0, 0), (lo, hi), (0, 0), (0, 0)))

    def windows(t):
        tr = to_classes(t, blk, Lp - L + blk).reshape(bsz, dil, nb + 2, blk, h, e)
        return jnp.concatenate([tr[:, :, :-2], tr[:, :, 1:-1], tr[:, :, 2:]], axis=3)

    qc = to_classes(q, 0, Lp - L).reshape(bsz, dil, nb, blk, h, e)
    kw = windows(k)
    vw = windows(v)
    ii = jnp.arange(blk)[:, None]
    tt = jnp.arange(3 * blk)[None, :]
    delta = tt - blk - ii
    m_k = jnp.arange(nb)[:, None, None] * blk - blk + tt[None]
    valid = (jnp.abs(delta) <= half)[None] & (m_k >= 0) & (m_k < L)
    scores = jnp.einsum('bdnihe,bdnthe->bdnhit', qc, kw) / math.sqrt(e)
    scores = jnp.where(valid[:, None], scores, NEG_INF)
    mx = jnp.max(scores, axis=-1, keepdims=True)
    pr = jnp.exp(scores - mx)
    den = jnp.sum(pr, axis=-1)
    o = jnp.einsum('bdnhit,bdnthe->bdnihe', pr, vw) / jnp.moveaxis(den, 3, 4)[..., None]
    lse = jnp.moveaxis(mx[..., 0] + jnp.log(den), 3, 4)

    def from_classes(t):
        tail = t.shape[4:]
        t = t.reshape(bsz, dil, Lp, *tail)[:, :, :L]
        return jnp.swapaxes(t, 1, 2).reshape(bsz, s, *tail)

    return from_classes(o), from_classes(lse)


def dilated_attention(q, k, v):
    outs = []
    lses = []
    for window, dil in DILATED_PATTERNS:
        o, lse = band_pattern(q, k, v, dil, window // (2 * dil))
        outs.append(o)
        lses.append(lse)
    w = jax.nn.softmax(jnp.stack(lses, axis=0), axis=0)
    return jnp.sum(w[..., None] * jnp.stack(outs, axis=0), axis=0)


def attention_mixer(u_q, u_k, u_v, q_g, k_g, out_g):
    bsz, s, _ = u_q.shape
    pos = jnp.arange(s)
    shp = (bsz, s, ATT_HEADS, HEAD_DIM)
    q = rope_partial(rmsnorm(u_q.reshape(shp).astype(jnp.float32), q_g), pos)
    k = rope_partial(rmsnorm(u_k.reshape(shp).astype(jnp.float32), k_g), pos)
    v = u_v.reshape(shp).astype(jnp.float32)
    o = dilated_attention(q, k, v).reshape(bsz, s, ATT_WIDTH)
    return rmsnorm(o, out_g)


def memory_cross_attention(hn, memn, wq, wkv, q_g, k_g, wo):
    bsz, s, _ = hn.shape
    m = memn.shape[1]
    q = (hn @ wq).reshape(bsz, s, XATT_HEADS, XATT_HEAD_DIM).astype(jnp.float32)
    kv = (memn @ wkv).reshape(bsz, m, 2, XATT_HEADS, XATT_HEAD_DIM).astype(jnp.float32)
    q = rmsnorm(q, q_g)
    k = rmsnorm(kv[:, :, 0], k_g)
    v = kv[:, :, 1]
    sc = jnp.einsum('bshe,bmhe->bhsm', q, k) / math.sqrt(XATT_HEAD_DIM)
    p = jax.nn.softmax(sc, axis=-1)
    o = jnp.einsum('bhsm,bmhe->bshe', p, v).reshape(bsz, s, D_MODEL)
    return o.astype(hn.dtype) @ wo


def encoder_layer(x, mem, p):
    u = rmsnorm(x, p['mix_norm_g']) @ p['w_in']
    ssd_out = ssd_mixer(u[..., OFF_Z:OFF_XBC], u[..., OFF_XBC:OFF_DT], u[..., OFF_DT:OFF_Q],
                        p['conv_w'], p['conv_b'], p['ssd_A_log'], p['ssd_dt_bias'], p['ssd_D'],
                        p['ssd_norm_g'])
    att_out = attention_mixer(u[..., OFF_Q:OFF_K], u[..., OFF_K:OFF_V], u[..., OFF_V:IN_COLS],
                              p['att_q_norm_g'], p['att_k_norm_g'], p['att_out_norm_g'])
    h = x + jnp.concatenate([ssd_out, att_out], axis=-1).astype(x.dtype) @ p['w_out']
    h = h + memory_cross_attention(rmsnorm(h, p['xatt_norm_g']), rmsnorm(mem, p['mem_norm_g']),
                                   p['xatt_wq'], p['xatt_wkv'], p['xatt_q_norm_g'],
                                   p['xatt_k_norm_g'], p['xatt_wo'])
    hm = rmsnorm(h, p['mlp_norm_g'])
    return h + jnp.square(jax.nn.relu(hm @ p['mlp_w1'])) @ p['mlp_w2']


def trunk(x, mem, params):
    for layer in range(DEPTH):
        x = encoder_layer(x, mem, {name: arr[layer] for name, arr in params.items()})
    return x


def setup_inputs(seed: int = 0) -> dict:
    key = jax.random.key(seed)
    ks = jax.random.split(key, 32)
    f32 = jnp.float32
    nrm = lambda k, shape, scale: jax.random.normal(k, shape, f32) * scale
    gain = lambda k, n: 1.0 + 0.02 * jax.random.normal(k, (DEPTH, n), f32)
    dt0 = jnp.exp(jax.random.uniform(ks[8], (DEPTH, 2, SSD_HEADS), f32, math.log(1e-3), math.log(1e-1)))
    return {
        'x_prompt': nrm(ks[0], (BATCH, SEQ, D_MODEL), 1.0),
        'x_sample': nrm(ks[1], (DEC_BATCH, DEC_SEQ, D_MODEL), 1.0),
        'mem_prompt': nrm(ks[2], (BATCH, MEM_LEN, D_MODEL), 1.0),
        'mem_sample': nrm(ks[3], (DEC_BATCH, MEM_LEN, D_MODEL), 1.0),
        'mix_norm_g': gain(ks[4], D_MODEL),
        'w_in': nrm(ks[5], (DEPTH, D_MODEL, IN_COLS), D_MODEL ** -0.5),
        'conv_w': nrm(ks[6], (DEPTH, SSD_CONV, CONV_CH), SSD_CONV ** -0.5),
        'conv_b': nrm(ks[7], (DEPTH, CONV_CH), 0.02),
        'ssd_A_log': jnp.log(jax.random.uniform(ks[9], (DEPTH, 2, SSD_HEADS), f32, 1.0, 16.0)),
        'ssd_dt_bias': dt0 + jnp.log(-jnp.expm1(-dt0)),
        'ssd_D': 1.0 + 0.02 * jax.random.normal(ks[10], (DEPTH, SSD_HEADS), f32),
        'ssd_norm_g': gain(ks[11], SSD_WIDTH),
        'att_q_norm_g': gain(ks[12], HEAD_DIM),
        'att_k_norm_g': gain(ks[13], HEAD_DIM),
        'att_out_norm_g': gain(ks[14], ATT_WIDTH),
        'w_out': nrm(ks[15], (DEPTH, MIX_WIDTH, D_MODEL), MIX_WIDTH ** -0.5),
        'xatt_norm_g': gain(ks[16], D_MODEL),
        'mem_norm_g': gain(ks[17], D_MODEL),
        'xatt_wq': nrm(ks[18], (DEPTH, D_MODEL, D_MODEL), D_MODEL ** -0.5),
        'xatt_wkv': nrm(ks[19], (DEPTH, D_MODEL, 2 * D_MODEL), D_MODEL ** -0.5),
        'xatt_q_norm_g': gain(ks[20], XATT_HEAD_DIM),
        'xatt_k_norm_g': gain(ks[21], XATT_HEAD_DIM),
        'xatt_wo': nrm(ks[22], (DEPTH, D_MODEL, D_MODEL), D_MODEL ** -0.5),
        'mlp_norm_g': gain(ks[23], D_MODEL),
        'mlp_w1': nrm(ks[24], (DEPTH, D_MODEL, D_FF), D_MODEL ** -0.5),
        'mlp_w2': nrm(ks[25], (DEPTH, D_FF, D_MODEL), D_FF ** -0.5),
    }


def reference(x_prompt, x_sample, mem_prompt, mem_sample, mix_norm_g, w_in, conv_w, conv_b,
              ssd_A_log, ssd_dt_bias, ssd_D, ssd_norm_g, att_q_norm_g, att_k_norm_g, att_out_norm_g,
              w_out, xatt_norm_g, mem_norm_g, xatt_wq, xatt_wkv, xatt_q_norm_g, xatt_k_norm_g,
              xatt_wo, mlp_norm_g, mlp_w1, mlp_w2):
    params = {
        'mix_norm_g': mix_norm_g, 'w_in': w_in, 'conv_w': conv_w, 'conv_b': conv_b,
        'ssd_A_log': ssd_A_log, 'ssd_dt_bias': ssd_dt_bias, 'ssd_D': ssd_D, 'ssd_norm_g': ssd_norm_g,
        'att_q_norm_g': att_q_norm_g, 'att_k_norm_g': att_k_norm_g, 'att_out_norm_g': att_out_norm_g,
        'w_out': w_out, 'xatt_norm_g': xatt_norm_g, 'mem_norm_g': mem_norm_g, 'xatt_wq': xatt_wq,
        'xatt_wkv': xatt_wkv, 'xatt_q_norm_g': xatt_q_norm_g, 'xatt_k_norm_g': xatt_k_norm_g,
        'xatt_wo': xatt_wo, 'mlp_norm_g': mlp_norm_g, 'mlp_w1': mlp_w1, 'mlp_w2': mlp_w2,
    }
    y_prompt = trunk(x_prompt, mem_prompt, params)
    y_sample = trunk(x_sample, mem_sample, params)
    return (y_prompt, y_sample)
```

```python
import functools
import math

import jax
import jax.numpy as jnp
from jax import lax
from jax.experimental import pallas as pl
from jax.experimental.pallas import tpu as pltpu

F32 = jnp.float32
BF16 = jnp.bfloat16

D_MODEL = 1024
HEAD_DIM = 64
SSD_HEADS = 8
SSD_WIDTH = SSD_HEADS * HEAD_DIM
SSD_GROUPS = 2
SSD_STATE = 128
SSD_CONV = 5
SSD_CHUNK = 128
ATT_HEADS = 8
ATT_WIDTH = ATT_HEADS * HEAD_DIM
DILATED_PATTERNS = ((128, 1), (512, 4), (2048, 16))
ROPE_DIMS = HEAD_DIM // 4
ROPE_THETA = 500000.0
MEM_LEN = 256
XATT_HEADS = 4
XATT_HEAD_DIM = D_MODEL // XATT_HEADS
D_FF = 4 * D_MODEL
EPS = 1e-6
NEG_INF = -1e30
CONV_CH = SSD_WIDTH + 2 * SSD_GROUPS * SSD_STATE
OFF_Z = 0
OFF_XBC = OFF_Z + SSD_WIDTH
OFF_DT = OFF_XBC + CONV_CH
OFF_Q = OFF_DT + 2 * SSD_HEADS
OFF_K = OFF_Q + ATT_WIDTH
OFF_V = OFF_K + ATT_WIDTH
IN_COLS = OFF_V + ATT_WIDTH

LANES = 128
BF16_ROWS = 16
VMEM_LIMIT = 56 * 1024 * 1024
BAND_HALF = 64
ATT_Q = 128
ATT_KEYS = ATT_Q + 2 * BAND_HALF


def _dot(a, b):
    return jnp.dot(a, b, preferred_element_type=F32)


def _dot_nt(a, b):
    return lax.dot_general(a, b, (((1,), (1,)), ((), ())), preferred_element_type=F32)


def _const_spec(shape):
    zeros = (0,) * len(shape)
    return pl.BlockSpec(shape, lambda *_: zeros, pipeline_mode=pl.Buffered(1))


def _params(sem):
    return pltpu.CompilerParams(dimension_semantics=sem, vmem_limit_bytes=VMEM_LIMIT)


def _lane(shape):
    return lax.broadcasted_iota(jnp.int32, shape, len(shape) - 1)


def _rms(x, g):
    ms = jnp.mean(x * x, axis=-1, keepdims=True)
    return x * lax.rsqrt(ms + EPS) * g


def _silu(x):
    return x * jax.nn.sigmoid(x)


def _inproj_kernel(x_ref, g_ref, wz_ref, wxbc_ref, wdt_ref, wq_ref, wk_ref, wv_ref, ones_ref,
                   qg_ref, kg_ref, cos_ref, sdn_ref, sup_ref,
                   z_out, xbc_out, dt_out, q_out, k_out, v_out):
    xn = _rms(x_ref[...], g_ref[...]).astype(BF16)
    z_out[...] = _dot(xn, wz_ref[...]).astype(BF16)
    xbc_out[...] = _dot(xn, wxbc_ref[...]).astype(BF16)
    dt_out[...] = _dot(xn, wdt_ref[...])
    v_out[...] = _dot(xn, wv_ref[...]).astype(BF16)

    cos = jnp.concatenate([cos_ref[...]] * 4, axis=1)
    sdn = jnp.concatenate([sdn_ref[...]] * 4, axis=1)
    sup = jnp.concatenate([sup_ref[...]] * 4, axis=1)

    def norm_rope(w_ref, gain_ref, scale):
        t = _dot(xn, w_ref[...])
        ssq = _dot((t * t).astype(BF16), ones_ref[...])
        t = t * lax.rsqrt(ssq * (1.0 / HEAD_DIM) + EPS) * gain_ref[...]
        half = ROPE_DIMS // 2
        t = (t * cos + pltpu.roll(t, half, 1) * sdn
             + pltpu.roll(t, ATT_WIDTH - half, 1) * sup)
        return (t * scale).astype(BF16)

    q_out[...] = norm_rope(wq_ref, qg_ref, 1.0 / math.sqrt(HEAD_DIM))
    k_out[...] = norm_rope(wk_ref, kg_ref, 1.0)


def _inproj(x2, seq, prm, tm):
    t_tokens = x2.shape[0]
    tiles_per_seq = seq // tm
    tok = lambda w: pl.BlockSpec((tm, w), lambda i: (i, 0))
    rope = pl.BlockSpec((tm, LANES), lambda i: (i % tiles_per_seq, 0))
    out_shape = (
        jax.ShapeDtypeStruct((t_tokens, SSD_WIDTH), BF16),
        jax.ShapeDtypeStruct((t_tokens, CONV_CH), BF16),
        jax.ShapeDtypeStruct((t_tokens, LANES), F32),
        jax.ShapeDtypeStruct((t_tokens, ATT_WIDTH), BF16),
        jax.ShapeDtypeStruct((t_tokens, ATT_WIDTH), BF16),
        jax.ShapeDtypeStruct((t_tokens, ATT_WIDTH), BF16),
    )
    return pl.pallas_call(
        _inproj_kernel,
        grid=(t_tokens // tm,),
        in_specs=[
            tok(D_MODEL), _const_spec((1, D_MODEL)),
            _const_spec((D_MODEL, SSD_WIDTH)), _const_spec((D_MODEL, CONV_CH)),
            _const_spec((D_MODEL, LANES)), _const_spec((D_MODEL, ATT_WIDTH)),
            _const_spec((D_MODEL, ATT_WIDTH)), _const_spec((D_MODEL, ATT_WIDTH)),
            _const_spec((ATT_WIDTH, ATT_WIDTH)),
            _const_spec((1, ATT_WIDTH)), _const_spec((1, ATT_WIDTH)),
            rope, rope, rope,
        ],
        out_specs=(tok(SSD_WIDTH), tok(CONV_CH), tok(LANES), tok(ATT_WIDTH), tok(ATT_WIDTH),
                   tok(ATT_WIDTH)),
        out_shape=out_shape,
        compiler_params=_params(("parallel",)),
        name="inproj",
    )(x2, prm["mix_g"], prm["w_z"], prm["w_xbc"], prm["w_dt"], prm["w_q"], prm["w_k"], prm["w_v"],
      prm["head_ones"], prm["qg"], prm["kg"], prm["rope_cos"][seq], prm["rope_sdn"][seq],
      prm["rope_sup"][seq])


def _split3(a):
    a1 = a.astype(BF16)
    r1 = a - a1.astype(F32)
    a2 = r1.astype(BF16)
    a3 = (r1 - a2.astype(F32)).astype(BF16)
    return a1, a2, a3


def _dot_split(lhs_bf16, a):
    a1, a2, a3 = _split3(a)
    return _dot(lhs_bf16, a1) + _dot(lhs_bf16, a2) + _dot(lhs_bf16, a3)


def _ssd_kernel(nc, xbc_ref, xprev_ref, xnext_ref, z_ref, dt_ref, convw_ref, convb_ref, alog_ref,
                dtbias_ref, drow_ref, ng_ref, out_ref, ext_ref, carry_ref, prevb_ref):
    q = SSD_CHUNK
    s = pl.program_id(1)
    backward = s < nc
    c = jnp.where(backward, nc - 1 - s, s - nc)

    keep_prev = jnp.where(c > 0, 1.0, 0.0)
    keep_next = jnp.where(c < nc - 1, 1.0, 0.0)
    ext_ref[0:BF16_ROWS, :] = xprev_ref[0].astype(F32) * keep_prev
    ext_ref[BF16_ROWS:BF16_ROWS + q, :] = xbc_ref[0].astype(F32)
    ext_ref[BF16_ROWS + q:, :] = xnext_ref[0].astype(F32) * keep_next

    def conv_silu(lo, hi):
        acc = convb_ref[:, lo:hi]
        pad = (SSD_CONV - 1) // 2
        for k in range(SSD_CONV):
            start = BF16_ROWS - pad + k
            acc = acc + ext_ref[start:start + q, lo:hi] * convw_ref[k:k + 1, lo:hi]
        return _silu(acc)

    lane = _lane((q, LANES))
    row = lax.broadcasted_iota(jnp.int32, (q, LANES), 0)
    nh = SSD_HEADS
    raw = dt_ref[0] + dtbias_ref[...]
    softplus = jnp.maximum(raw, 0.0) + jnp.log(1.0 + jnp.exp(-jnp.abs(raw)))
    dt = jnp.where(lane < 2 * nh, softplus, 0.0)
    a = dt * (-jnp.exp(alog_ref[...]))

    erow = lax.broadcasted_iota(jnp.int32, (LANES, 2 * SSD_WIDTH), 0)
    ecol = lax.broadcasted_iota(jnp.int32, (LANES, 2 * SSD_WIDTH), 1)
    expand = (erow == ecol // HEAD_DIM).astype(BF16)

    def expand_rows(v, lo, hi):
        return _dot(v.astype(BF16), expand[:, lo:hi])

    def expand_row_exact(v, lo, hi):
        v1, v2, v3 = _split3(jnp.broadcast_to(v, (8, LANES)))
        e = expand[:, lo:hi]
        return (_dot(v1, e) + _dot(v2, e) + _dot(v3, e))[0:1]

    ltri = (lane <= row).astype(BF16)
    utri = (lane >= row).astype(BF16)

    @pl.when(s == 0)
    def _():
        carry_ref[...] = jnp.zeros_like(carry_ref)

    @pl.when(s == nc)
    def _():
        carry_ref[...] = jnp.zeros_like(carry_ref)

    def chunk_states(xs, bm, w_full, lo):
        xw = (xs * w_full).astype(BF16)
        parts = []
        for g in range(SSD_GROUPS):
            bt = bm[:, g * SSD_STATE:(g + 1) * SSD_STATE].T.astype(BF16)
            width = SSD_WIDTH // SSD_GROUPS
            parts.append(_dot(bt, xw[:, g * width:(g + 1) * width]))
        return jnp.concatenate(parts, axis=1)

    @pl.when(backward)
    def _():
        xb = conv_silu(0, SSD_WIDTH + SSD_GROUPS * SSD_STATE)
        xs = xb[:, :SSD_WIDTH]
        bm = xb[:, SSD_WIDTH:]
        sb = _dot_split(utri, a)
        bmask = (lane >= nh) & (lane < 2 * nh)
        wb = jnp.where(bmask, jnp.exp(sb[0:1, :] - sb) * dt, 0.0)
        w_full = expand_rows(wb, SSD_WIDTH, 2 * SSD_WIDTH)
        states = chunk_states(xs, bm, w_full, SSD_WIDTH)
        decay = expand_row_exact(jnp.where(bmask[0:1], jnp.exp(sb[0:1, :]), 0.0),
                                 SSD_WIDTH, 2 * SSD_WIDTH)
        prevb_ref[c] = carry_ref[...].astype(BF16)
        carry_ref[...] = carry_ref[...] * decay + states

    @pl.when(jnp.logical_not(backward))
    def _():
        xb = conv_silu(0, CONV_CH)
        xs = xb[:, :SSD_WIDTH]
        bm = xb[:, SSD_WIDTH:SSD_WIDTH + SSD_GROUPS * SSD_STATE]
        cm = xb[:, SSD_WIDTH + SSD_GROUPS * SSD_STATE:]
        cf = _dot_split(ltri, a)
        sb = _dot_split(utri, a)
        pack = jnp.where(lane < nh, cf, jnp.where(lane < 2 * nh, sb, pltpu.roll(dt, 2 * nh, 1)))
        pack_t = pack.T

        cb = [_dot_nt(cm[:, g * SSD_STATE:(g + 1) * SSD_STATE].astype(BF16),
                      bm[:, g * SSD_STATE:(g + 1) * SSD_STATE].astype(BF16))
              for g in range(SSD_GROUPS)]
        ii = lax.broadcasted_iota(jnp.int32, (q, q), 0)
        jj = lax.broadcasted_iota(jnp.int32, (q, q), 1)
        mats = []
        for h in range(nh):
            cf_col, cf_row = pack[:, h:h + 1], pack_t[h:h + 1, :]
            sb_col, sb_row = pack[:, nh + h:nh + h + 1], pack_t[nh + h:nh + h + 1, :]
            dtf_row = pack_t[2 * nh + h:2 * nh + h + 1, :]
            dtb_row = pack_t[3 * nh + h:3 * nh + h + 1, :]
            expo = jnp.where(jj <= ii, cf_col - cf_row, sb_col - sb_row)
            dtm = jnp.where(jj < ii, dtf_row, jnp.where(jj > ii, dtb_row, dtf_row + dtb_row))
            mats.append((cb[h // (nh // SSD_GROUPS)] * jnp.exp(expo) * dtm).astype(BF16))

        lane_p = _lane((q, LANES))
        y_parts = []
        for p in range(nh // 2):
            xp = xs[:, p * LANES:(p + 1) * LANES].astype(BF16)
            zero = jnp.zeros_like(xp)
            rhs = jnp.concatenate([jnp.where(lane_p < HEAD_DIM, xp, zero),
                                   jnp.where(lane_p >= HEAD_DIM, xp, zero)], axis=0)
            lhs = jnp.concatenate([mats[2 * p], mats[2 * p + 1]], axis=1)
            y_parts.append(_dot(lhs, rhs))
        y = jnp.concatenate(y_parts, axis=1)

        edge = expand_rows(jnp.where(lane < 2 * nh, jnp.exp(pack), 0.0), 0, 2 * SSD_WIDTH)
        prev_f = carry_ref[...].astype(BF16)
        prev_b = prevb_ref[c]
        width = SSD_WIDTH // SSD_GROUPS
        off_parts = []
        for g in range(SSD_GROUPS):
            cg = cm[:, g * SSD_STATE:(g + 1) * SSD_STATE].astype(BF16)
            st = jnp.concatenate([prev_f[:, g * width:(g + 1) * width],
                                  prev_b[:, g * width:(g + 1) * width]], axis=1)
            res = _dot(cg, st)
            off_parts.append(res[:, :width] * edge[:, g * width:(g + 1) * width]
                             + res[:, width:] * edge[:, SSD_WIDTH + g * width:SSD_WIDTH + (g + 1) * width])
        y = y + jnp.concatenate(off_parts, axis=1) + drow_ref[...] * xs

        y = y * _silu(z_ref[0].astype(F32))
        out_ref[0] = _rms(y, ng_ref[...]).astype(BF16)

        fmask = lane < nh
        wf = jnp.where(fmask, jnp.exp(cf[q - 1:q, :] - cf) * dt, 0.0)
        w_full = expand_rows(wf, 0, SSD_WIDTH)
        states = chunk_states(xs, bm, w_full, 0)
        decay = expand_row_exact(jnp.where(fmask[0:1], jnp.exp(cf[q - 1:q, :]), 0.0), 0, SSD_WIDTH)
        carry_ref[...] = carry_ref[...] * decay + states


def _ssd(z, xbc, dt, prm, bsz, seq):
    q = SSD_CHUNK
    nc = seq // q
    halo_blocks = q // BF16_ROWS
    chunk_of = lambda s: jnp.where(s < nc, nc - 1 - s, s - nc)
    z3 = z.reshape(bsz, seq, SSD_WIDTH)
    xbc3 = xbc.reshape(bsz, seq, CONV_CH)
    dt3 = dt.reshape(bsz, seq, LANES)
    in_specs = [
        pl.BlockSpec((1, q, CONV_CH), lambda b, s: (b, chunk_of(s), 0)),
        pl.BlockSpec((1, BF16_ROWS, CONV_CH),
                     lambda b, s: (b, jnp.maximum(chunk_of(s) * halo_blocks - 1, 0), 0)),
        pl.BlockSpec((1, BF16_ROWS, CONV_CH),
                     lambda b, s: (b, jnp.minimum((chunk_of(s) + 1) * halo_blocks,
                                                  seq // BF16_ROWS - 1), 0)),
        pl.BlockSpec((1, q, SSD_WIDTH), lambda b, s: (b, chunk_of(s), 0)),
        pl.BlockSpec((1, q, LANES), lambda b, s: (b, chunk_of(s), 0)),
        _const_spec((SSD_CONV, CONV_CH)), _const_spec((1, CONV_CH)),
        _const_spec((1, LANES)), _const_spec((1, LANES)),
        _const_spec((1, SSD_WIDTH)), _const_spec((1, SSD_WIDTH)),
    ]
    out_spec = pl.BlockSpec((1, q, SSD_WIDTH), lambda b, s: (b, jnp.maximum(s - nc, 0), 0))
    out = pl.pallas_call(
        functools.partial(_ssd_kernel, nc),
        grid=(bsz, 2 * nc),
        in_specs=in_specs,
        out_specs=out_spec,
        out_shape=jax.ShapeDtypeStruct((bsz, seq, SSD_WIDTH), BF16),
        scratch_shapes=[
            pltpu.VMEM((q + 2 * BF16_ROWS, CONV_CH), F32),
            pltpu.VMEM((SSD_STATE, SSD_WIDTH), F32),
            pltpu.VMEM((nc, SSD_STATE, SSD_WIDTH), BF16),
        ],
        compiler_params=_params(("parallel", "arbitrary")),
        name="ssd",
    )(xbc3, xbc3, xbc3, z3, dt3, prm["conv_w"], prm["conv_b"], prm["alog_row"], prm["dtbias_row"],
      prm["d_row"], prm["ssd_ng"])
    return out.reshape(bsz * seq, SSD_WIDTH)


def _attn_kernel(cls_len, blk, has_prev, is_last, *refs):
    refs = list(refs)
    q_ref, kc_ref, kp_ref, kn_ref, vc_ref, vp_ref, vn_ref = refs[:7]
    refs = refs[7:]
    if has_prev:
        oprev_ref, lprev_ref = refs[:2]
        refs = refs[2:]
    if is_last:
        og_ref = refs[0]
        refs = refs[1:]
        o_out = refs[0]
        refs = refs[1:]
    else:
        o_out, lse_out = refs[:2]
        refs = refs[2:]
    kwin_ref, vwin_ref = refs

    n = pl.program_id(2)
    nblk = cls_len // blk
    half = BAND_HALF
    kwin_ref[0:half, :] = kp_ref[0]
    kwin_ref[half:half + blk, :] = kc_ref[0]
    kwin_ref[half + blk:, :] = kn_ref[0]
    vwin_ref[0:half, :] = vp_ref[0]
    vwin_ref[half:half + blk, :] = vc_ref[0]
    vwin_ref[half + blk:, :] = vn_ref[0]

    rows2 = 2 * ATT_Q
    qi = lax.broadcasted_iota(jnp.int32, (rows2, ATT_KEYS), 0) % ATT_Q
    kt = lax.broadcasted_iota(jnp.int32, (rows2, ATT_KEYS), 1)
    band = jnp.where(jnp.abs(kt - half - qi) <= half, 0.0, NEG_INF)
    kt1 = lax.broadcasted_iota(jnp.int32, (1, ATT_KEYS), 1)
    first_bias = jnp.where((n == 0) & (kt1 < half), NEG_INF, 0.0)
    last_bias = jnp.where((n == nblk - 1) & (kt1 >= half + ATT_Q), NEG_INF, 0.0)

    lane = _lane((ATT_Q, LANES))
    low = lane < HEAD_DIM
    ones = jnp.ones((ATT_KEYS, LANES), BF16)
    nsub = blk // ATT_Q
    npair = ATT_HEADS // 2

    for j in range(nsub):
        r0 = j * ATT_Q
        if has_prev:
            stat_prev = lprev_ref[0, r0:r0 + ATT_Q, :]
        stat_new = jnp.zeros((ATT_Q, LANES), F32)
        o_tiles = []
        for c in range(npair):
            l0 = c * LANES
            qp = q_ref[0, r0:r0 + ATT_Q, l0:l0 + LANES]
            zero = jnp.zeros_like(qp)
            lhs = jnp.concatenate([jnp.where(low, qp, zero), jnp.where(low, zero, qp)], axis=0)
            sc = _dot_nt(lhs, kwin_ref[r0:r0 + ATT_KEYS, l0:l0 + LANES]) + band
            if j == 0:
                sc = sc + first_bias
            if j == nsub - 1:
                sc = sc + last_bias
            mx = jnp.max(sc, axis=-1, keepdims=True)
            pr = jnp.exp(sc - mx).astype(BF16)
            vcat = jnp.concatenate([vwin_ref[r0:r0 + ATT_KEYS, l0:l0 + LANES], ones], axis=1)
            res = _dot(pr, vcat)
            num = jnp.where(low, res[:ATT_Q, :LANES], res[ATT_Q:, :LANES])
            den = jnp.where(low, res[:ATT_Q, LANES:], res[ATT_Q:, LANES:])
            mxp = jnp.where(low, jnp.broadcast_to(mx[:ATT_Q], (ATT_Q, LANES)),
                            jnp.broadcast_to(mx[ATT_Q:], (ATT_Q, LANES)))
            o_new = num / den
            lse_new = mxp + jnp.log(den)
            if has_prev:
                lp = jnp.where(low, jnp.broadcast_to(stat_prev[:, c:c + 1], (ATT_Q, LANES)),
                               jnp.broadcast_to(stat_prev[:, HEAD_DIM + c:HEAD_DIM + c + 1],
                                                (ATT_Q, LANES)))
                top = jnp.maximum(lp, lse_new)
                e_old = jnp.exp(lp - top)
                e_new = jnp.exp(lse_new - top)
                tot = e_old + e_new
                o_old = oprev_ref[0, r0:r0 + ATT_Q, l0:l0 + LANES]
                o_new = (o_old * e_old + o_new * e_new) / tot
                lse_new = top + jnp.log(tot)
            if is_last:
                o_tiles.append(o_new)
            else:
                o_out[0, r0:r0 + ATT_Q, l0:l0 + LANES] = o_new
                stat_new = jnp.where(lane % HEAD_DIM == c, lse_new, stat_new)
        if is_last:
            o_full = jnp.concatenate(o_tiles, axis=1)
            o_out[0, r0:r0 + ATT_Q, :] = _rms(o_full, og_ref[...]).astype(BF16)
        else:
            lse_out[0, r0:r0 + ATT_Q, :] = stat_new


def _attn_pattern(q, k, v, state, out_g, dil, bsz, seq, is_last):
    cls_len = seq // dil
    blk = min(512, cls_len)
    assert cls_len % blk == 0 and blk % ATT_Q == 0 and cls_len >= 2 * ATT_Q
    nblk = cls_len // blk
    halo_per_blk = blk // BAND_HALF
    has_prev = state is not None
    view = lambda t, w: t.reshape(bsz, cls_len, dil * w)
    cur = lambda w: pl.BlockSpec((1, blk, w), lambda b, r, n: (b, n, r))
    prev_halo = pl.BlockSpec((1, BAND_HALF, ATT_WIDTH),
                             lambda b, r, n: (b, jnp.maximum(n * halo_per_blk - 1, 0), r))
    next_halo = pl.BlockSpec((1, BAND_HALF, ATT_WIDTH),
                             lambda b, r, n: (b, jnp.minimum((n + 1) * halo_per_blk,
                                                             cls_len // BAND_HALF - 1), r))
    qv, kv, vv = view(q, ATT_WIDTH), view(k, ATT_WIDTH), view(v, ATT_WIDTH)
    args = [qv, kv, kv, kv, vv, vv, vv]
    in_specs = [cur(ATT_WIDTH), cur(ATT_WIDTH), prev_halo, next_halo, cur(ATT_WIDTH), prev_halo,
                next_halo]
    if has_prev:
        args += [view(state[0], ATT_WIDTH), view(state[1], LANES)]
        in_specs += [cur(ATT_WIDTH), cur(LANES)]
    if is_last:
        args.append(out_g)
        in_specs.append(_const_spec((1, ATT_WIDTH)))
        out_shape = jax.ShapeDtypeStruct((bsz, cls_len, dil * ATT_WIDTH), BF16)
        out_specs = cur(ATT_WIDTH)
    else:
        out_shape = (jax.ShapeDtypeStruct((bsz, cls_len, dil * ATT_WIDTH), F32),
                     jax.ShapeDtypeStruct((bsz, cls_len, dil * LANES), F32))
        out_specs = (cur(ATT_WIDTH), cur(LANES))
    out = pl.pallas_call(
        functools.partial(_attn_kernel, cls_len, blk, has_prev, is_last),
        grid=(bsz, dil, nblk),
        in_specs=in_specs,
        out_specs=out_specs,
        out_shape=out_shape,
        scratch_shapes=[pltpu.VMEM((blk + 2 * BAND_HALF, ATT_WIDTH), BF16),
                        pltpu.VMEM((blk + 2 * BAND_HALF, ATT_WIDTH), BF16)],
        compiler_params=_params(("parallel", "parallel", "parallel")),
        name=f"attn_d{dil}",
    )(*args)
    if is_last:
        return out.reshape(bsz * seq, ATT_WIDTH)
    return out[0].reshape(bsz * seq, ATT_WIDTH), out[1].reshape(bsz * seq, LANES)


def _dilated_attention(q, k, v, out_g, bsz, seq):
    state = None
    for idx, (window, dil) in enumerate(DILATED_PATTERNS):
        assert window // (2 * dil) == BAND_HALF
        is_last = idx == len(DILATED_PATTERNS) - 1
        state = _attn_pattern(q, k, v, state, out_g, dil, bsz, seq, is_last)
    return state


def _memkv_kernel(mem_ref, g_ref, wkv_ref, kg_ref, kt_out, v_out):
    mn = _rms(mem_ref[0], g_ref[...]).astype(BF16)
    kv = _dot(mn, wkv_ref[...])
    scale = 1.0 / math.sqrt(XATT_HEAD_DIM)
    for h in range(XATT_HEADS):
        kh = kv[:, h * XATT_HEAD_DIM:(h + 1) * XATT_HEAD_DIM]
        kh = _rms(kh, kg_ref[...]) * scale
        kt_out[0, h * XATT_HEAD_DIM:(h + 1) * XATT_HEAD_DIM, :] = kh.T.astype(BF16)
    v_out[0] = kv[:, D_MODEL:].astype(BF16)


def _memkv(mem, prm):
    bsz = mem.shape[0]
    return pl.pallas_call(
        _memkv_kernel,
        grid=(bsz,),
        in_specs=[pl.BlockSpec((1, MEM_LEN, D_MODEL), lambda b: (b, 0, 0)),
                  _const_spec((1, D_MODEL)), _const_spec((D_MODEL, 2 * D_MODEL)),
                  _const_spec((1, XATT_HEAD_DIM))],
        out_specs=(pl.BlockSpec((1, D_MODEL, MEM_LEN), lambda b: (b, 0, 0)),
                   pl.BlockSpec((1, MEM_LEN, D_MODEL), lambda b: (b, 0, 0))),
        out_shape=(jax.ShapeDtypeStruct((bsz, D_MODEL, MEM_LEN), BF16),
                   jax.ShapeDtypeStruct((bsz, MEM_LEN, D_MODEL), BF16)),
        compiler_params=_params(("parallel",)),
        name="memkv",
    )(mem, prm["mem_g"], prm["w_kv"], prm["xkg"])


def _post_kernel(x_ref, ssd_ref, att_ref, wo1_ref, wo2_ref, g_ref, wq_ref, qg_ref, kt_ref, v_ref,
                 wo_ref, h_out):
    h = x_ref[...] + _dot(ssd_ref[...], wo1_ref[...]) + _dot(att_ref[...], wo2_ref[...])
    hn = _rms(h, g_ref[...]).astype(BF16)
    qf = _dot(hn, wq_ref[...])
    outs = []
    for hd in range(XATT_HEADS):
        sl = slice(hd * XATT_HEAD_DIM, (hd + 1) * XATT_HEAD_DIM)
        qh = _rms(qf[:, sl], qg_ref[...]).astype(BF16)
        sc = _dot(qh, kt_ref[0, sl, :])
        mx = jnp.max(sc, axis=-1, keepdims=True)
        pr = jnp.exp(sc - mx)
        den = jnp.sum(pr, axis=-1, keepdims=True)
        outs.append(_dot(pr.astype(BF16), v_ref[0, :, sl]) / den)
    o = jnp.concatenate(outs, axis=1).astype(BF16)
    h_out[...] = h + _dot(o, wo_ref[...])


def _post(x2, ssd, att, kt, vx, prm, seq, tm):
    t_tokens = x2.shape[0]
    tiles_per_seq = seq // tm
    tok = lambda w: pl.BlockSpec((tm, w), lambda i: (i, 0))
    half = D_MODEL // 2
    return pl.pallas_call(
        _post_kernel,
        grid=(t_tokens // tm,),
        in_specs=[tok(D_MODEL), tok(SSD_WIDTH), tok(ATT_WIDTH),
                  _const_spec((half, D_MODEL)), _const_spec((half, D_MODEL)),
                  _const_spec((1, D_MODEL)), _const_spec((D_MODEL, D_MODEL)),
                  _const_spec((1, XATT_HEAD_DIM)),
                  pl.BlockSpec((1, D_MODEL, MEM_LEN), lambda i: (i // tiles_per_seq, 0, 0)),
                  pl.BlockSpec((1, MEM_LEN, D_MODEL), lambda i: (i // tiles_per_seq, 0, 0)),
                  _const_spec((D_MODEL, D_MODEL))],
        out_specs=tok(D_MODEL),
        out_shape=jax.ShapeDtypeStruct((t_tokens, D_MODEL), F32),
        compiler_params=_params(("parallel",)),
        name="post",
    )(x2, ssd, att, prm["w_out1"], prm["w_out2"], prm["xatt_g"], prm["w_xq"], prm["xqg"], kt, vx,
      prm["w_xo"])


def _mlp_kernel(ff_chunk, h_ref, g_ref, w1_ref, w2_ref, y_out):
    h = h_ref[...]
    hm = _rms(h, g_ref[...]).astype(BF16)
    acc = h
    for c in range(D_FF // ff_chunk):
        sl = slice(c * ff_chunk, (c + 1) * ff_chunk)
        a = jnp.maximum(_dot(hm, w1_ref[:, sl]), 0.0)
        acc = acc + _dot((a * a).astype(BF16), w2_ref[sl, :])
    y_out[...] = acc


def _mlp(h, prm, tm):
    t_tokens = h.shape[0]
    tok = pl.BlockSpec((tm, D_MODEL), lambda i: (i, 0))
    return pl.pallas_call(
        functools.partial(_mlp_kernel, 1024),
        grid=(t_tokens // tm,),
        in_specs=[tok, _const_spec((1, D_MODEL)), _const_spec((D_MODEL, D_FF)),
                  _const_spec((D_FF, D_MODEL))],
        out_specs=tok,
        out_shape=jax.ShapeDtypeStruct((t_tokens, D_MODEL), F32),
        compiler_params=_params(("parallel",)),
        name="mlp",
    )(h, prm["mlp_g"], prm["w1"], prm["w2"])


def _rope_tables(seq):
    half = ROPE_DIMS // 2
    inv_freq = jnp.power(jnp.float32(ROPE_THETA), -jnp.arange(half, dtype=F32) / half)
    ang = jnp.arange(seq).astype(F32)[:, None] * inv_freq[None, :]
    cos, sin = jnp.cos(ang), jnp.sin(ang)
    zeros = jnp.zeros((seq, half), F32)
    rest = HEAD_DIM - ROPE_DIMS
    one_head = lambda a, b, fill: jnp.concatenate([a, b, jnp.full((seq, rest), fill, F32)], axis=1)
    two = lambda t: jnp.concatenate([t, t], axis=1)
    return (two(one_head(cos, cos, 1.0)), two(one_head(zeros, sin, 0.0)),
            two(one_head(-sin, zeros, 0.0)))


def _prepare(seqs, mix_norm_g, w_in, conv_w, conv_b, ssd_A_log, ssd_dt_bias, ssd_D, ssd_norm_g,
             att_q_norm_g, att_k_norm_g, att_out_norm_g, w_out, xatt_norm_g, mem_norm_g, xatt_wq,
             xatt_wkv, xatt_q_norm_g, xatt_k_norm_g, xatt_wo, mlp_norm_g, mlp_w1, mlp_w2):
    row = lambda t: t.reshape(1, -1).astype(F32)
    pad_lanes = lambda t: jnp.pad(t, ((0, 0), (0, LANES - t.shape[1])))
    head_id = jnp.arange(ATT_WIDTH) // HEAD_DIM
    prm = {
        "mix_g": row(mix_norm_g),
        "w_z": w_in[:, OFF_Z:OFF_XBC].astype(BF16),
        "w_xbc": w_in[:, OFF_XBC:OFF_DT].astype(BF16),
        "w_dt": pad_lanes(w_in[:, OFF_DT:OFF_Q]).astype(BF16),
        "w_q": w_in[:, OFF_Q:OFF_K].astype(BF16),
        "w_k": w_in[:, OFF_K:OFF_V].astype(BF16),
        "w_v": w_in[:, OFF_V:IN_COLS].astype(BF16),
        "head_ones": (head_id[:, None] == head_id[None, :]).astype(BF16),
        "qg": jnp.tile(row(att_q_norm_g), (1, ATT_HEADS)),
        "kg": jnp.tile(row(att_k_norm_g), (1, ATT_HEADS)),
        "conv_w": conv_w.astype(F32),
        "conv_b": row(conv_b),
        "alog_row": pad_lanes(row(ssd_A_log)),
        "dtbias_row": pad_lanes(row(ssd_dt_bias)),
        "d_row": jnp.repeat(row(ssd_D), HEAD_DIM, axis=1),
        "ssd_ng": row(ssd_norm_g),
        "att_og": row(att_out_norm_g),
        "w_out1": w_out[:SSD_WIDTH].astype(BF16),
        "w_out2": w_out[SSD_WIDTH:].astype(BF16),
        "xatt_g": row(xatt_norm_g),
        "mem_g": row(mem_norm_g),
        "w_xq": xatt_wq.astype(BF16),
        "w_kv": xatt_wkv.astype(BF16),
        "xqg": row(xatt_q_norm_g),
        "xkg": row(xatt_k_norm_g),
        "w_xo": xatt_wo.astype(BF16),
        "mlp_g": row(mlp_norm_g),
        "w1": mlp_w1.astype(BF16),
        "w2": mlp_w2.astype(BF16),
        "rope_cos": {}, "rope_sdn": {}, "rope_sup": {},
    }
    for seq in set(seqs):
        cos, sdn, sup = _rope_tables(seq)
        prm["rope_cos"][seq], prm["rope_sdn"][seq], prm["rope_sup"][seq] = cos, sdn, sup
    return prm


def _token_tile(seq):
    return min(512, seq)


def _layer(x, mem, prm):
    bsz, seq, _ = x.shape
    tm = _token_tile(seq)
    x2 = x.reshape(bsz * seq, D_MODEL)
    z, xbc, dt, q, k, v = _inproj(x2, seq, prm, tm)
    ssd = _ssd(z, xbc, dt, prm, bsz, seq)
    att = _dilated_attention(q, k, v, prm["att_og"], bsz, seq)
    kt, vx = _memkv(mem, prm)
    h = _post(x2, ssd, att, kt, vx, prm, seq, tm)
    y = _mlp(h, prm, tm)
    return y.reshape(bsz, seq, D_MODEL)


def kernel(x_prompt, x_sample, mem_prompt, mem_sample, mix_norm_g, w_in, conv_w, conv_b, ssd_A_log,
           ssd_dt_bias, ssd_D, ssd_norm_g, att_q_norm_g, att_k_norm_g, att_out_norm_g, w_out,
           xatt_norm_g, mem_norm_g, xatt_wq, xatt_wkv, xatt_q_norm_g, xatt_k_norm_g, xatt_wo,
           mlp_norm_g, mlp_w1, mlp_w2):
    weights = (mix_norm_g, w_in, conv_w, conv_b, ssd_A_log, ssd_dt_bias, ssd_D, ssd_norm_g,
               att_q_norm_g, att_k_norm_g, att_out_norm_g, w_out, xatt_norm_g, mem_norm_g, xatt_wq,
               xatt_wkv, xatt_q_norm_g, xatt_k_norm_g, xatt_wo, mlp_norm_g, mlp_w1, mlp_w2)
    assert all(w.shape[0] == 1 for w in weights), "single-layer stack expected"
    prm = _prepare((x_prompt.shape[1], x_sample.shape[1]), *(w[0] for w in weights))
    return (_layer(x_prompt, mem_prompt, prm), _layer(x_sample, mem_sample, prm))
```

```python
import functools
import math

import jax
import jax.numpy as jnp
from jax import lax
from jax.experimental import pallas as pl
from jax.experimental.pallas import tpu as pltpu

F32 = jnp.float32
BF16 = jnp.bfloat16

D_MODEL = 1024
HEAD_DIM = 64
SSD_HEADS = 8
SSD_WIDTH = SSD_HEADS * HEAD_DIM
SSD_GROUPS = 2
SSD_STATE = 128
SSD_CONV = 5
SSD_CHUNK = 128
ATT_HEADS = 8
ATT_WIDTH = ATT_HEADS * HEAD_DIM
DILATED_PATTERNS = ((128, 1), (512, 4), (2048, 16))
ROPE_DIMS = HEAD_DIM // 4
ROPE_THETA = 500000.0
MEM_LEN = 256
XATT_HEADS = 4
XATT_HEAD_DIM = D_MODEL // XATT_HEADS
D_FF = 4 * D_MODEL
EPS = 1e-6
NEG_INF = -1e30
CONV_CH = SSD_WIDTH + 2 * SSD_GROUPS * SSD_STATE
OFF_Z = 0
OFF_XBC = OFF_Z + SSD_WIDTH
OFF_DT = OFF_XBC + CONV_CH
OFF_Q = OFF_DT + 2 * SSD_HEADS
OFF_K = OFF_Q + ATT_WIDTH
OFF_V = OFF_K + ATT_WIDTH
IN_COLS = OFF_V + ATT_WIDTH

LANES = 128
BF16_ROWS = 16
VMEM_LIMIT = 56 * 1024 * 1024
BAND_HALF = 64
ATT_Q = 128
ATT_KEYS = ATT_Q + 2 * BAND_HALF
ATT_BLK = 2048


def _dot(a, b):
    return jnp.dot(a, b, preferred_element_type=F32)


def _dot_nt(a, b):
    return lax.dot_general(a, b, (((1,), (1,)), ((), ())), preferred_element_type=F32)


def _const_spec(shape):
    zeros = (0,) * len(shape)
    return pl.BlockSpec(shape, lambda *_: zeros, pipeline_mode=pl.Buffered(1))


def _params(sem):
    return pltpu.CompilerParams(dimension_semantics=sem, vmem_limit_bytes=VMEM_LIMIT)


def _lane(shape):
    return lax.broadcasted_iota(jnp.int32, shape, len(shape) - 1)


def _rms(x, g):
    ms = jnp.mean(x * x, axis=-1, keepdims=True)
    return x * lax.rsqrt(ms + EPS) * g


def _silu(x):
    return x * jax.nn.sigmoid(x)


def _emit_layouts(t, nat_out, c4_out, c16_out, slab_ref, slab2_ref):
    tm = t.shape[0]
    n4, n16 = tm // 4, tm // 16
    nat_out[...] = t.astype(BF16)
    for p in range(ATT_WIDTH // LANES):
        sl = slice(p * LANES, (p + 1) * LANES)
        slab_ref[p] = t[:, sl]
        for r4 in range(4):
            c4 = slab_ref[p, pl.ds(r4, n4, stride=4), :]
            c4_out[0, r4, :, sl] = c4.astype(BF16)
            slab2_ref[p, r4 * n4:(r4 + 1) * n4, :] = c4
        for r4 in range(4):
            for a in range(4):
                c16 = slab2_ref[p, pl.ds(r4 * n4 + a, n16, stride=4), :]
                c16_out[0, 4 * a + r4, :, sl] = c16.astype(BF16)


def _inproj_kernel(x_ref, g_ref, wz_ref, wxbc_ref, wdt_ref, wq_ref, wk_ref, wv_ref, ones_ref,
                   qg_ref, kg_ref, cos_ref, sdn_ref, sup_ref,
                   z_out, xbc_out, dt_out, q1_out, q4_out, q16_out, k1_out, k4_out, k16_out,
                   v1_out, v4_out, v16_out, slab_ref, slab2_ref):
    xn = _rms(x_ref[...], g_ref[...]).astype(BF16)
    z_out[...] = _dot(xn, wz_ref[...]).astype(BF16)
    xbc_out[...] = _dot(xn, wxbc_ref[...]).astype(BF16)
    dt_out[...] = _dot(xn, wdt_ref[...])
    _emit_layouts(_dot(xn, wv_ref[...]), v1_out, v4_out, v16_out, slab_ref, slab2_ref)

    cos = jnp.concatenate([cos_ref[...]] * 4, axis=1)
    sdn = jnp.concatenate([sdn_ref[...]] * 4, axis=1)
    sup = jnp.concatenate([sup_ref[...]] * 4, axis=1)

    def norm_rope(w_ref, gain_ref, scale):
        t = _dot(xn, w_ref[...])
        ssq = _dot((t * t).astype(BF16), ones_ref[...])
        t = t * lax.rsqrt(ssq * (1.0 / HEAD_DIM) + EPS) * gain_ref[...]
        half = ROPE_DIMS // 2
        t = (t * cos + pltpu.roll(t, half, 1) * sdn
             + pltpu.roll(t, ATT_WIDTH - half, 1) * sup)
        return t * scale

    _emit_layouts(norm_rope(wq_ref, qg_ref, 1.0 / math.sqrt(HEAD_DIM)), q1_out, q4_out, q16_out,
                  slab_ref, slab2_ref)
    _emit_layouts(norm_rope(wk_ref, kg_ref, 1.0), k1_out, k4_out, k16_out, slab_ref, slab2_ref)


def _inproj(x2, bsz, seq, prm, tm):
    t_tokens = x2.shape[0]
    tiles_per_seq = seq // tm
    tok = lambda w: pl.BlockSpec((tm, w), lambda i: (i, 0))
    rope = pl.BlockSpec((tm, LANES), lambda i: (i % tiles_per_seq, 0))
    cls = lambda d: pl.BlockSpec((1, d, tm // d, ATT_WIDTH),
                                 lambda i: (i // tiles_per_seq, 0, i % tiles_per_seq, 0))
    cls_shape = lambda d: jax.ShapeDtypeStruct((bsz, d, seq // d, ATT_WIDTH), BF16)
    qkv_shapes = (jax.ShapeDtypeStruct((t_tokens, ATT_WIDTH), BF16), cls_shape(4), cls_shape(16))
    qkv_specs = (tok(ATT_WIDTH), cls(4), cls(16))
    out_shape = (
        jax.ShapeDtypeStruct((t_tokens, SSD_WIDTH), BF16),
        jax.ShapeDtypeStruct((t_tokens, CONV_CH), BF16),
        jax.ShapeDtypeStruct((t_tokens, LANES), F32),
    ) + qkv_shapes * 3
    return pl.pallas_call(
        _inproj_kernel,
        grid=(t_tokens // tm,),
        in_specs=[
            tok(D_MODEL), _const_spec((1, D_MODEL)),
            _const_spec((D_MODEL, SSD_WIDTH)), _const_spec((D_MODEL, CONV_CH)),
            _const_spec((D_MODEL, LANES)), _const_spec((D_MODEL, ATT_WIDTH)),
            _const_spec((D_MODEL, ATT_WIDTH)), _const_spec((D_MODEL, ATT_WIDTH)),
            _const_spec((ATT_WIDTH, ATT_WIDTH)),
            _const_spec((1, ATT_WIDTH)), _const_spec((1, ATT_WIDTH)),
            rope, rope, rope,
        ],
        out_specs=(tok(SSD_WIDTH), tok(CONV_CH), tok(LANES)) + qkv_specs * 3,
        out_shape=out_shape,
        scratch_shapes=[pltpu.VMEM((ATT_WIDTH // LANES, tm, LANES), F32),
                        pltpu.VMEM((ATT_WIDTH // LANES, tm, LANES), F32)],
        compiler_params=_params(("parallel",)),
        name="inproj",
    )(x2, prm["mix_g"], prm["w_z"], prm["w_xbc"], prm["w_dt"], prm["w_q"], prm["w_k"], prm["w_v"],
      prm["head_ones"], prm["qg"], prm["kg"], prm["rope_cos"][seq], prm["rope_sdn"][seq],
      prm["rope_sup"][seq])


def _split3(a):
    a1 = a.astype(BF16)
    r1 = a - a1.astype(F32)
    a2 = r1.astype(BF16)
    a3 = (r1 - a2.astype(F32)).astype(BF16)
    return a1, a2, a3


def _dot_split(lhs_bf16, a):
    a1, a2, a3 = _split3(a)
    return _dot(lhs_bf16, a1) + _dot(lhs_bf16, a2) + _dot(lhs_bf16, a3)


def _ssd_kernel(nc, xbc_ref, xprev_ref, xnext_ref, z_ref, dt_ref, convw_ref, convb_ref, alog_ref,
                dtbias_ref, drow_ref, ng_ref, out_ref, ext_ref, carry_ref, prevb_ref):
    q = SSD_CHUNK
    s = pl.program_id(1)
    backward = s < nc
    c = jnp.where(backward, nc - 1 - s, s - nc)

    keep_prev = jnp.where(c > 0, 1.0, 0.0)
    keep_next = jnp.where(c < nc - 1, 1.0, 0.0)
    ext_ref[0:BF16_ROWS, :] = xprev_ref[0].astype(F32) * keep_prev
    ext_ref[BF16_ROWS:BF16_ROWS + q, :] = xbc_ref[0].astype(F32)
    ext_ref[BF16_ROWS + q:, :] = xnext_ref[0].astype(F32) * keep_next

    def conv_silu(lo, hi):
        acc = convb_ref[:, lo:hi]
        pad = (SSD_CONV - 1) // 2
        for k in range(SSD_CONV):
            start = BF16_ROWS - pad + k
            acc = acc + ext_ref[start:start + q, lo:hi] * convw_ref[k:k + 1, lo:hi]
        return _silu(acc)

    lane = _lane((q, LANES))
    row = lax.broadcasted_iota(jnp.int32, (q, LANES), 0)
    nh = SSD_HEADS
    raw = dt_ref[0] + dtbias_ref[...]
    softplus = jnp.maximum(raw, 0.0) + jnp.log(1.0 + jnp.exp(-jnp.abs(raw)))
    dt = jnp.where(lane < 2 * nh, softplus, 0.0)
    a = dt * (-jnp.exp(alog_ref[...]))

    erow = lax.broadcasted_iota(jnp.int32, (LANES, 2 * SSD_WIDTH), 0)
    ecol = lax.broadcasted_iota(jnp.int32, (LANES, 2 * SSD_WIDTH), 1)
    expand = (erow == ecol // HEAD_DIM).astype(BF16)

    def expand_rows(v, lo, hi):
        return _dot(v.astype(BF16), expand[:, lo:hi])

    def expand_row_exact(v, lo, hi):
        v1, v2, v3 = _split3(jnp.broadcast_to(v, (8, LANES)))
        e = expand[:, lo:hi]
        return (_dot(v1, e) + _dot(v2, e) + _dot(v3, e))[0:1]

    ltri = (lane <= row).astype(BF16)
    utri = (lane >= row).astype(BF16)

    @pl.when(s == 0)
    def _():
        carry_ref[...] = jnp.zeros_like(carry_ref)

    @pl.when(s == nc)
    def _():
        carry_ref[...] = jnp.zeros_like(carry_ref)

    def chunk_states(xs, bm, w_full, lo):
        xw = (xs * w_full).astype(BF16)
        parts = []
        for g in range(SSD_GROUPS):
            bt = bm[:, g * SSD_STATE:(g + 1) * SSD_STATE].T.astype(BF16)
            width = SSD_WIDTH // SSD_GROUPS
            parts.append(_dot(bt, xw[:, g * width:(g + 1) * width]))
        return jnp.concatenate(parts, axis=1)

    @pl.when(backward)
    def _():
        xb = conv_silu(0, SSD_WIDTH + SSD_GROUPS * SSD_STATE)
        xs = xb[:, :SSD_WIDTH]
        bm = xb[:, SSD_WIDTH:]
        sb = _dot_split(utri, a)
        bmask = (lane >= nh) & (lane < 2 * nh)
        wb = jnp.where(bmask, jnp.exp(sb[0:1, :] - sb) * dt, 0.0)
        w_full = expand_rows(wb, SSD_WIDTH, 2 * SSD_WIDTH)
        states = chunk_states(xs, bm, w_full, SSD_WIDTH)
        decay = expand_row_exact(jnp.where(bmask[0:1], jnp.exp(sb[0:1, :]), 0.0),
                                 SSD_WIDTH, 2 * SSD_WIDTH)
        prevb_ref[c] = carry_ref[...].astype(BF16)
        carry_ref[...] = carry_ref[...] * decay + states

    @pl.when(jnp.logical_not(backward))
    def _():
        xb = conv_silu(0, CONV_CH)
        xs = xb[:, :SSD_WIDTH]
        bm = xb[:, SSD_WIDTH:SSD_WIDTH + SSD_GROUPS * SSD_STATE]
        cm = xb[:, SSD_WIDTH + SSD_GROUPS * SSD_STATE:]
        cf = _dot_split(ltri, a)
        sb = _dot_split(utri, a)
        pack = jnp.where(lane < nh, cf, jnp.where(lane < 2 * nh, sb, pltpu.roll(dt, 2 * nh, 1)))
        pack_t = pack.T

        cb = [_dot_nt(cm[:, g * SSD_STATE:(g + 1) * SSD_STATE].astype(BF16),
                      bm[:, g * SSD_STATE:(g + 1) * SSD_STATE].astype(BF16))
              for g in range(SSD_GROUPS)]
        ii = lax.broadcasted_iota(jnp.int32, (q, q), 0)
        jj = lax.broadcasted_iota(jnp.int32, (q, q), 1)
        mats = []
        for h in range(nh):
            cf_col, cf_row = pack[:, h:h + 1], pack_t[h:h + 1, :]
            sb_col, sb_row = pack[:, nh + h:nh + h + 1], pack_t[nh + h:nh + h + 1, :]
            dtf_row = pack_t[2 * nh + h:2 * nh + h + 1, :]
            dtb_row = pack_t[3 * nh + h:3 * nh + h + 1, :]
            expo = jnp.where(jj <= ii, cf_col - cf_row, sb_col - sb_row)
            dtm = jnp.where(jj < ii, dtf_row, jnp.where(jj > ii, dtb_row, dtf_row + dtb_row))
            mats.append((cb[h // (nh // SSD_GROUPS)] * jnp.exp(expo) * dtm).astype(BF16))

        lane_p = _lane((q, LANES))
        y_parts = []
        for p in range(nh // 2):
            xp = xs[:, p * LANES:(p + 1) * LANES].astype(BF16)
            zero = jnp.zeros_like(xp)
            rhs = jnp.concatenate([jnp.where(lane_p < HEAD_DIM, xp, zero),
                                   jnp.where(lane_p >= HEAD_DIM, xp, zero)], axis=0)
            lhs = jnp.concatenate([mats[2 * p], mats[2 * p + 1]], axis=1)
            y_parts.append(_dot(lhs, rhs))
        y = jnp.concatenate(y_parts, axis=1)

        edge = expand_rows(jnp.where(lane < 2 * nh, jnp.exp(pack), 0.0), 0, 2 * SSD_WIDTH)
        prev_f = carry_ref[...].astype(BF16)
        prev_b = prevb_ref[c]
        width = SSD_WIDTH // SSD_GROUPS
        off_parts = []
        for g in range(SSD_GROUPS):
            cg = cm[:, g * SSD_STATE:(g + 1) * SSD_STATE].astype(BF16)
            st = jnp.concatenate([prev_f[:, g * width:(g + 1) * width],
                                  prev_b[:, g * width:(g + 1) * width]], axis=1)
            res = _dot(cg, st)
            off_parts.append(res[:, :width] * edge[:, g * width:(g + 1) * width]
                             + res[:, width:] * edge[:, SSD_WIDTH + g * width:SSD_WIDTH + (g + 1) * width])
        y = y + jnp.concatenate(off_parts, axis=1) + drow_ref[...] * xs

        y = y * _silu(z_ref[0].astype(F32))
        out_ref[0] = _rms(y, ng_ref[...]).astype(BF16)

        fmask = lane < nh
        wf = jnp.where(fmask, jnp.exp(cf[q - 1:q, :] - cf) * dt, 0.0)
        w_full = expand_rows(wf, 0, SSD_WIDTH)
        states = chunk_states(xs, bm, w_full, 0)
        decay = expand_row_exact(jnp.where(fmask[0:1], jnp.exp(cf[q - 1:q, :]), 0.0), 0, SSD_WIDTH)
        carry_ref[...] = carry_ref[...] * decay + states


def _ssd(z, xbc, dt, prm, bsz, seq):
    q = SSD_CHUNK
    nc = seq // q
    halo_blocks = q // BF16_ROWS
    chunk_of = lambda s: jnp.where(s < nc, nc - 1 - s, s - nc)
    z3 = z.reshape(bsz, seq, SSD_WIDTH)
    xbc3 = xbc.reshape(bsz, seq, CONV_CH)
    dt3 = dt.reshape(bsz, seq, LANES)
    in_specs = [
        pl.BlockSpec((1, q, CONV_CH), lambda b, s: (b, chunk_of(s), 0)),
        pl.BlockSpec((1, BF16_ROWS, CONV_CH),
                     lambda b, s: (b, jnp.maximum(chunk_of(s) * halo_blocks - 1, 0), 0)),
        pl.BlockSpec((1, BF16_ROWS, CONV_CH),
                     lambda b, s: (b, jnp.minimum((chunk_of(s) + 1) * halo_blocks,
                                                  seq // BF16_ROWS - 1), 0)),
        pl.BlockSpec((1, q, SSD_WIDTH), lambda b, s: (b, chunk_of(s), 0)),
        pl.BlockSpec((1, q, LANES), lambda b, s: (b, chunk_of(s), 0)),
        _const_spec((SSD_CONV, CONV_CH)), _const_spec((1, CONV_CH)),
        _const_spec((1, LANES)), _const_spec((1, LANES)),
        _const_spec((1, SSD_WIDTH)), _const_spec((1, SSD_WIDTH)),
    ]
    out_spec = pl.BlockSpec((1, q, SSD_WIDTH), lambda b, s: (b, jnp.maximum(s - nc, 0), 0))
    out = pl.pallas_call(
        functools.partial(_ssd_kernel, nc),
        grid=(bsz, 2 * nc),
        in_specs=in_specs,
        out_specs=out_spec,
        out_shape=jax.ShapeDtypeStruct((bsz, seq, SSD_WIDTH), BF16),
        scratch_shapes=[
            pltpu.VMEM((q + 2 * BF16_ROWS, CONV_CH), F32),
            pltpu.VMEM((SSD_STATE, SSD_WIDTH), F32),
            pltpu.VMEM((nc, SSD_STATE, SSD_WIDTH), BF16),
        ],
        compiler_params=_params(("parallel", "arbitrary")),
        name="ssd",
    )(xbc3, xbc3, xbc3, z3, dt3, prm["conv_w"], prm["conv_b"], prm["alog_row"], prm["dtbias_row"],
      prm["d_row"], prm["ssd_ng"])
    return out.reshape(bsz * seq, SSD_WIDTH)


def _attn_unit(q_tile, kwin_ref, vwin_ref, row0, bias):
    low = _lane((ATT_Q, LANES)) < HEAD_DIM
    zero = jnp.zeros_like(q_tile)
    lhs = jnp.concatenate([jnp.where(low, q_tile, zero), jnp.where(low, zero, q_tile)], axis=0)
    sc = _dot_nt(lhs, kwin_ref[pl.ds(row0, ATT_KEYS), :]) + bias
    mx = jnp.max(sc, axis=-1, keepdims=True)
    pr = jnp.exp(sc - mx).astype(BF16)
    ones = jnp.ones((ATT_KEYS, LANES), BF16)
    vcat = jnp.concatenate([vwin_ref[pl.ds(row0, ATT_KEYS), :], ones], axis=1)
    res = _dot(pr, vcat)
    num = jnp.where(low, res[:ATT_Q, :LANES], res[ATT_Q:, :LANES])
    den = jnp.where(low, res[:ATT_Q, LANES:], res[ATT_Q:, LANES:])
    mxp = jnp.where(low, jnp.broadcast_to(mx[:ATT_Q], (ATT_Q, LANES)),
                    jnp.broadcast_to(mx[ATT_Q:], (ATT_Q, LANES)))
    return num / den, mxp + jnp.log(den)


def _attn_merge(o_old, l_old, o_new, l_new):
    top = jnp.maximum(l_old, l_new)
    e_old = jnp.exp(l_old - top)
    e_new = jnp.exp(l_new - top)
    tot = e_old + e_new
    return (o_old * e_old + o_new * e_new) / tot, top + jnp.log(tot)


def _attn_kernel(nblk, q1, k1c, k1p, k1n, v1c, v1p, v1n, q4, k4c, k4p, k4n, v4c, v4p, v4n,
                 q16, k16c, k16p, k16n, v16c, v16p, v16n, o_out, kwin, vwin, oacc, lacc, bias_ref):
    n = pl.program_id(2)
    half = BAND_HALF
    rows2 = 2 * ATT_Q
    qi = lax.broadcasted_iota(jnp.int32, (rows2, ATT_KEYS), 0) % ATT_Q
    kt = lax.broadcasted_iota(jnp.int32, (rows2, ATT_KEYS), 1)
    band = jnp.where(jnp.abs(kt - half - qi) <= half, 0.0, NEG_INF)
    bias_ref[0] = band
    bias_ref[1] = jnp.where(kt < half, NEG_INF, band)
    bias_ref[2] = jnp.where(kt >= half + ATT_Q, NEG_INF, band)

    def bias_for(first_sub, last_sub):
        idx = jnp.where((n == 0) & first_sub, 1, jnp.where((n == nblk - 1) & last_sub, 2, 0))
        return bias_ref[idx]

    def fill(win, prev, cur, nxt, rows):
        win[0:half, :] = prev
        win[half:half + rows, :] = cur
        win[half + rows:2 * half + rows, :] = nxt

    def class16(r, carry):
        fill(kwin, k16p[0, r], k16c[0, r], k16n[0, r], ATT_Q)
        fill(vwin, v16p[0, r], v16c[0, r], v16n[0, r], ATT_Q)
        o_p, l_p = _attn_unit(q16[0, r], kwin, vwin, 0, bias_for(True, True))
        oacc[pl.ds(r, ATT_Q, stride=16), :] = o_p
        lacc[pl.ds(r, ATT_Q, stride=16), :] = l_p
        return carry

    lax.fori_loop(0, 16, class16, 0)

    rows4 = ATT_BLK // 4
    nsub4 = rows4 // ATT_Q

    def class4(r, carry):
        fill(kwin, k4p[0, r], k4c[0, r], k4n[0, r], rows4)
        fill(vwin, v4p[0, r], v4c[0, r], v4n[0, r], rows4)
        for j in range(nsub4):
            o_p, l_p = _attn_unit(q4[0, r, j * ATT_Q:(j + 1) * ATT_Q, :], kwin, vwin, j * ATT_Q,
                                  bias_for(j == 0, j == nsub4 - 1))
            rows = pl.ds(r + 4 * ATT_Q * j, ATT_Q, stride=4)
            o_m, l_m = _attn_merge(oacc[rows, :], lacc[rows, :], o_p, l_p)
            oacc[rows, :] = o_m
            lacc[rows, :] = l_m
        return carry

    lax.fori_loop(0, 4, class4, 0)

    fill(kwin, k1p[0], k1c[0], k1n[0], ATT_BLK)
    fill(vwin, v1p[0], v1c[0], v1n[0], ATT_BLK)
    nsub1 = ATT_BLK // ATT_Q

    def sub1(j, carry):
        r0 = pl.multiple_of(j * ATT_Q, ATT_Q)
        rows = pl.ds(r0, ATT_Q)
        o_p, l_p = _attn_unit(q1[0, rows, :], kwin, vwin, r0, bias_for(j == 0, j == nsub1 - 1))
        o_m, _ = _attn_merge(oacc[rows, :], lacc[rows, :], o_p, l_p)
        o_out[0, rows, :] = o_m.astype(BF16)
        return carry

    lax.fori_loop(0, nsub1, sub1, 0)


def _dilated_attention(qkv, bsz, seq):
    (q1, q4, q16), (k1, k4, k16), (v1, v4, v16) = qkv
    assert tuple(d for _, d in DILATED_PATTERNS) == (1, 4, 16)
    assert all(w // (2 * d) == BAND_HALF for w, d in DILATED_PATTERNS)
    assert seq % ATT_BLK == 0 and seq // 16 >= 2 * ATT_Q
    nblk = seq // ATT_BLK
    npair = ATT_WIDTH // LANES
    nat = lambda t: t.reshape(bsz, seq, ATT_WIDTH)

    def specs(dil):
        rows = ATT_BLK // dil
        per_blk = rows // BAND_HALF
        last = seq // dil // BAND_HALF - 1
        if dil == 1:
            cur = pl.BlockSpec((1, rows, LANES), lambda b, p, n: (b, n, p))
            prev = pl.BlockSpec((1, BAND_HALF, LANES),
                                lambda b, p, n: (b, jnp.maximum(n * per_blk - 1, 0), p))
            nxt = pl.BlockSpec((1, BAND_HALF, LANES),
                               lambda b, p, n: (b, jnp.minimum((n + 1) * per_blk, last), p))
        else:
            cur = pl.BlockSpec((1, dil, rows, LANES), lambda b, p, n: (b, 0, n, p))
            prev = pl.BlockSpec((1, dil, BAND_HALF, LANES),
                                lambda b, p, n: (b, 0, jnp.maximum(n * per_blk - 1, 0), p))
            nxt = pl.BlockSpec((1, dil, BAND_HALF, LANES),
                               lambda b, p, n: (b, 0, jnp.minimum((n + 1) * per_blk, last), p))
        return [cur, cur, prev, nxt, cur, prev, nxt]

    args = ([nat(q1)] + [nat(k1)] * 3 + [nat(v1)] * 3 + [q4] + [k4] * 3 + [v4] * 3
            + [q16] + [k16] * 3 + [v16] * 3)
    out = pl.pallas_call(
        functools.partial(_attn_kernel, nblk),
        grid=(bsz, npair, nblk),
        in_specs=specs(1) + specs(4) + specs(16),
        out_specs=pl.BlockSpec((1, ATT_BLK, LANES), lambda b, p, n: (b, n, p)),
        out_shape=jax.ShapeDtypeStruct((bsz, seq, ATT_WIDTH), BF16),
        scratch_shapes=[pltpu.VMEM((ATT_BLK + 2 * BAND_HALF, LANES), BF16),
                        pltpu.VMEM((ATT_BLK + 2 * BAND_HALF, LANES), BF16),
                        pltpu.VMEM((ATT_BLK, LANES), F32),
                        pltpu.VMEM((ATT_BLK, LANES), F32),
                        pltpu.VMEM((3, 2 * ATT_Q, ATT_KEYS), F32)],
        compiler_params=_params(("parallel", "parallel", "parallel")),
        name="attn",
    )(*args)
    return out.reshape(bsz * seq, ATT_WIDTH)


def _memkv_kernel(mem_ref, g_ref, wkv_ref, kg_ref, kt_out, v_out):
    mn = _rms(mem_ref[0], g_ref[...]).astype(BF16)
    kv = _dot(mn, wkv_ref[...])
    scale = 1.0 / math.sqrt(XATT_HEAD_DIM)
    for h in range(XATT_HEADS):
        kh = kv[:, h * XATT_HEAD_DIM:(h + 1) * XATT_HEAD_DIM]
        kh = _rms(kh, kg_ref[...]) * scale
        kt_out[0, h * XATT_HEAD_DIM:(h + 1) * XATT_HEAD_DIM, :] = kh.T.astype(BF16)
    v_out[0] = kv[:, D_MODEL:].astype(BF16)


def _memkv(mem, prm):
    bsz = mem.shape[0]
    return pl.pallas_call(
        _memkv_kernel,
        grid=(bsz,),
        in_specs=[pl.BlockSpec((1, MEM_LEN, D_MODEL), lambda b: (b, 0, 0)),
                  _const_spec((1, D_MODEL)), _const_spec((D_MODEL, 2 * D_MODEL)),
                  _const_spec((1, XATT_HEAD_DIM))],
        out_specs=(pl.BlockSpec((1, D_MODEL, MEM_LEN), lambda b: (b, 0, 0)),
                   pl.BlockSpec((1, MEM_LEN, D_MODEL), lambda b: (b, 0, 0))),
        out_shape=(jax.ShapeDtypeStruct((bsz, D_MODEL, MEM_LEN), BF16),
                   jax.ShapeDtypeStruct((bsz, MEM_LEN, D_MODEL), BF16)),
        compiler_params=_params(("parallel",)),
        name="memkv",
    )(mem, prm["mem_g"], prm["w_kv"], prm["xkg"])


def _post_kernel(x_ref, ssd_ref, att_ref, og_ref, wo1_ref, wo2_ref, g_ref, wq_ref, qg_ref, kt_ref,
                 v_ref, wo_ref, h_out):
    att = _rms(att_ref[...].astype(F32), og_ref[...]).astype(BF16)
    h = x_ref[...] + _dot(ssd_ref[...], wo1_ref[...]) + _dot(att, wo2_ref[...])
    hn = _rms(h, g_ref[...]).astype(BF16)
    qf = _dot(hn, wq_ref[...])
    outs = []
    for hd in range(XATT_HEADS):
        sl = slice(hd * XATT_HEAD_DIM, (hd + 1) * XATT_HEAD_DIM)
        qh = _rms(qf[:, sl], qg_ref[...]).astype(BF16)
        sc = _dot(qh, kt_ref[0, sl, :])
        mx = jnp.max(sc, axis=-1, keepdims=True)
        pr = jnp.exp(sc - mx)
        den = jnp.sum(pr, axis=-1, keepdims=True)
        outs.append(_dot(pr.astype(BF16), v_ref[0, :, sl]) / den)
    o = jnp.concatenate(outs, axis=1).astype(BF16)
    h_out[...] = h + _dot(o, wo_ref[...])


def _post(x2, ssd, att, kt, vx, prm, seq, tm):
    t_tokens = x2.shape[0]
    tiles_per_seq = seq // tm
    tok = lambda w: pl.BlockSpec((tm, w), lambda i: (i, 0))
    half = D_MODEL // 2
    return pl.pallas_call(
        _post_kernel,
        grid=(t_tokens // tm,),
        in_specs=[tok(D_MODEL), tok(SSD_WIDTH), tok(ATT_WIDTH), _const_spec((1, ATT_WIDTH)),
                  _const_spec((half, D_MODEL)), _const_spec((half, D_MODEL)),
                  _const_spec((1, D_MODEL)), _const_spec((D_MODEL, D_MODEL)),
                  _const_spec((1, XATT_HEAD_DIM)),
                  pl.BlockSpec((1, D_MODEL, MEM_LEN), lambda i: (i // tiles_per_seq, 0, 0)),
                  pl.BlockSpec((1, MEM_LEN, D_MODEL), lambda i: (i // tiles_per_seq, 0, 0)),
                  _const_spec((D_MODEL, D_MODEL))],
        out_specs=tok(D_MODEL),
        out_shape=jax.ShapeDtypeStruct((t_tokens, D_MODEL), F32),
        compiler_params=_params(("parallel",)),
        name="post",
    )(x2, ssd, att, prm["att_og"], prm["w_out1"], prm["w_out2"], prm["xatt_g"], prm["w_xq"], prm["xqg"], kt, vx,
      prm["w_xo"])


def _mlp_kernel(ff_chunk, h_ref, g_ref, w1_ref, w2_ref, y_out):
    h = h_ref[...]
    hm = _rms(h, g_ref[...]).astype(BF16)
    acc = h
    for c in range(D_FF // ff_chunk):
        sl = slice(c * ff_chunk, (c + 1) * ff_chunk)
        a = jnp.maximum(_dot(hm, w1_ref[:, sl]), 0.0)
        acc = acc + _dot((a * a).astype(BF16), w2_ref[sl, :])
    y_out[...] = acc


def _mlp(h, prm, tm):
    t_tokens = h.shape[0]
    tok = pl.BlockSpec((tm, D_MODEL), lambda i: (i, 0))
    return pl.pallas_call(
        functools.partial(_mlp_kernel, 1024),
        grid=(t_tokens // tm,),
        in_specs=[tok, _const_spec((1, D_MODEL)), _const_spec((D_MODEL, D_FF)),
                  _const_spec((D_FF, D_MODEL))],
        out_specs=tok,
        out_shape=jax.ShapeDtypeStruct((t_tokens, D_MODEL), F32),
        compiler_params=_params(("parallel",)),
        name="mlp",
    )(h, prm["mlp_g"], prm["w1"], prm["w2"])


def _rope_tables(seq):
    half = ROPE_DIMS // 2
    inv_freq = jnp.power(jnp.float32(ROPE_THETA), -jnp.arange(half, dtype=F32) / half)
    ang = jnp.arange(seq).astype(F32)[:, None] * inv_freq[None, :]
    cos, sin = jnp.cos(ang), jnp.sin(ang)
    zeros = jnp.zeros((seq, half), F32)
    rest = HEAD_DIM - ROPE_DIMS
    one_head = lambda a, b, fill: jnp.concatenate([a, b, jnp.full((seq, rest), fill, F32)], axis=1)
    two = lambda t: jnp.concatenate([t, t], axis=1)
    return (two(one_head(cos, cos, 1.0)), two(one_head(zeros, sin, 0.0)),
            two(one_head(-sin, zeros, 0.0)))


def _prepare(seqs, mix_norm_g, w_in, conv_w, conv_b, ssd_A_log, ssd_dt_bias, ssd_D, ssd_norm_g,
             att_q_norm_g, att_k_norm_g, att_out_norm_g, w_out, xatt_norm_g, mem_norm_g, xatt_wq,
             xatt_wkv, xatt_q_norm_g, xatt_k_norm_g, xatt_wo, mlp_norm_g, mlp_w1, mlp_w2):
    row = lambda t: t.reshape(1, -1).astype(F32)
    pad_lanes = lambda t: jnp.pad(t, ((0, 0), (0, LANES - t.shape[1])))
    head_id = jnp.arange(ATT_WIDTH) // HEAD_DIM
    prm = {
        "mix_g": row(mix_norm_g),
        "w_z": w_in[:, OFF_Z:OFF_XBC].astype(BF16),
        "w_xbc": w_in[:, OFF_XBC:OFF_DT].astype(BF16),
        "w_dt": pad_lanes(w_in[:, OFF_DT:OFF_Q]).astype(BF16),
        "w_q": w_in[:, OFF_Q:OFF_K].astype(BF16),
        "w_k": w_in[:, OFF_K:OFF_V].astype(BF16),
        "w_v": w_in[:, OFF_V:IN_COLS].astype(BF16),
        "head_ones": (head_id[:, None] == head_id[None, :]).astype(BF16),
        "qg": jnp.tile(row(att_q_norm_g), (1, ATT_HEADS)),
        "kg": jnp.tile(row(att_k_norm_g), (1, ATT_HEADS)),
        "conv_w": conv_w.astype(F32),
        "conv_b": row(conv_b),
        "alog_row": pad_lanes(row(ssd_A_log)),
        "dtbias_row": pad_lanes(row(ssd_dt_bias)),
        "d_row": jnp.repeat(row(ssd_D), HEAD_DIM, axis=1),
        "ssd_ng": row(ssd_norm_g),
        "att_og": row(att_out_norm_g),
        "w_out1": w_out[:SSD_WIDTH].astype(BF16),
        "w_out2": w_out[SSD_WIDTH:].astype(BF16),
        "xatt_g": row(xatt_norm_g),
        "mem_g": row(mem_norm_g),
        "w_xq": xatt_wq.astype(BF16),
        "w_kv": xatt_wkv.astype(BF16),
        "xqg": row(xatt_q_norm_g),
        "xkg": row(xatt_k_norm_g),
        "w_xo": xatt_wo.astype(BF16),
        "mlp_g": row(mlp_norm_g),
        "w1": mlp_w1.astype(BF16),
        "w2": mlp_w2.astype(BF16),
        "rope_cos": {}, "rope_sdn": {}, "rope_sup": {},
    }
    for seq in set(seqs):
        cos, sdn, sup = _rope_tables(seq)
        prm["rope_cos"][seq], prm["rope_sdn"][seq], prm["rope_sup"][seq] = cos, sdn, sup
    return prm


def _token_tile(seq):
    return min(512, seq)


def _layer(x, mem, prm):
    bsz, seq, _ = x.shape
    tm = _token_tile(seq)
    x2 = x.reshape(bsz * seq, D_MODEL)
    outs = _inproj(x2, bsz, seq, prm, tm)
    z, xbc, dt = outs[:3]
    ssd = _ssd(z, xbc, dt, prm, bsz, seq)
    att = _dilated_attention((outs[3:6], outs[6:9], outs[9:12]), bsz, seq)
    kt, vx = _memkv(mem, prm)
    h = _post(x2, ssd, att, kt, vx, prm, seq, tm)
    y = _mlp(h, prm, tm)
    return y.reshape(bsz, seq, D_MODEL)


def kernel(x_prompt, x_sample, mem_prompt, mem_sample, mix_norm_g, w_in, conv_w, conv_b, ssd_A_log,
           ssd_dt_bias, ssd_D, ssd_norm_g, att_q_norm_g, att_k_norm_g, att_out_norm_g, w_out,
           xatt_norm_g, mem_norm_g, xatt_wq, xatt_wkv, xatt_q_norm_g, xatt_k_norm_g, xatt_wo,
           mlp_norm_g, mlp_w1, mlp_w2):
    weights = (mix_norm_g, w_in, conv_w, conv_b, ssd_A_log, ssd_dt_bias, ssd_D, ssd_norm_g,
               att_q_norm_g, att_k_norm_g, att_out_norm_g, w_out, xatt_norm_g, mem_norm_g, xatt_wq,
               xatt_wkv, xatt_q_norm_g, xatt_k_norm_g, xatt_wo, mlp_norm_g, mlp_w1, mlp_w2)
    assert all(w.shape[0] == 1 for w in weights), "single-layer stack expected"
    prm = _prepare((x_prompt.shape[1], x_sample.shape[1]), *(w[0] for w in weights))
    return (_layer(x_prompt, mem_prompt, prm), _layer(x_sample, mem_sample, prm))
```

```python
import functools
import math

import jax
import jax.numpy as jnp
from jax import lax
from jax.experimental import pallas as pl
from jax.experimental.pallas import tpu as pltpu

F32 = jnp.float32
BF16 = jnp.bfloat16

D_MODEL = 1024
HEAD_DIM = 64
SSD_HEADS = 8
SSD_WIDTH = SSD_HEADS * HEAD_DIM
SSD_GROUPS = 2
SSD_STATE = 128
SSD_CONV = 5
SSD_CHUNK = 128
ATT_HEADS = 8
ATT_WIDTH = ATT_HEADS * HEAD_DIM
DILATED_PATTERNS = ((128, 1), (512, 4), (2048, 16))
ROPE_DIMS = HEAD_DIM // 4
ROPE_THETA = 500000.0
MEM_LEN = 256
XATT_HEADS = 4
XATT_HEAD_DIM = D_MODEL // XATT_HEADS
D_FF = 4 * D_MODEL
EPS = 1e-6
NEG_INF = -1e30
CONV_CH = SSD_WIDTH + 2 * SSD_GROUPS * SSD_STATE
OFF_Z = 0
OFF_XBC = OFF_Z + SSD_WIDTH
OFF_DT = OFF_XBC + CONV_CH
OFF_Q = OFF_DT + 2 * SSD_HEADS
OFF_K = OFF_Q + ATT_WIDTH
OFF_V = OFF_K + ATT_WIDTH
IN_COLS = OFF_V + ATT_WIDTH

LANES = 128
BF16_ROWS = 16
VMEM_LIMIT = 56 * 1024 * 1024
SSD_CPS = 4
BAND_HALF = 64
ATT_Q = 128
ATT_KEYS = ATT_Q + 2 * BAND_HALF
ATT_BLK = 2048
ATT_GROUP = 16
LOG2E = math.log2(math.e)


def _dot(a, b):
    return jnp.dot(a, b, preferred_element_type=F32)


def _dot_nt(a, b):
    return lax.dot_general(a, b, (((1,), (1,)), ((), ())), preferred_element_type=F32)


def _const_spec(shape):
    zeros = (0,) * len(shape)
    return pl.BlockSpec(shape, lambda *_: zeros, pipeline_mode=pl.Buffered(1))


def _params(sem):
    return pltpu.CompilerParams(dimension_semantics=sem, vmem_limit_bytes=VMEM_LIMIT)


def _lane(shape):
    return lax.broadcasted_iota(jnp.int32, shape, len(shape) - 1)


def _rms(x, g):
    ms = jnp.mean(x * x, axis=-1, keepdims=True)
    return x * lax.rsqrt(ms + EPS) * g


def _silu(x):
    return x * jax.nn.sigmoid(x)


def _emit_layouts(t, nat_out, c4_out, c16_out, slab_ref, slab2_ref):
    tm = t.shape[0]
    n4, n16 = tm // 4, tm // 16
    nat_out[...] = t.astype(BF16)
    for p in range(ATT_WIDTH // LANES):
        sl = slice(p * LANES, (p + 1) * LANES)
        slab_ref[p] = t[:, sl]
        for r4 in range(4):
            c4 = slab_ref[p, pl.ds(r4, n4, stride=4), :]
            c4_out[0, r4, :, sl] = c4.astype(BF16)
            slab2_ref[p, r4 * n4:(r4 + 1) * n4, :] = c4
        for r4 in range(4):
            for a in range(4):
                c16 = slab2_ref[p, pl.ds(r4 * n4 + a, n16, stride=4), :]
                c16_out[0, 4 * a + r4, :, sl] = c16.astype(BF16)


def _inproj_kernel(x_ref, g_ref, wz_ref, wxbc_ref, wdt_ref, wq_ref, wk_ref, wv_ref, ones_ref,
                   qg_ref, kg_ref, cos_ref, sdn_ref, sup_ref,
                   z_out, xbc_out, dt_out, q1_out, q4_out, q16_out, k1_out, k4_out, k16_out,
                   v1_out, v4_out, v16_out, slab_ref, slab2_ref):
    xn = _rms(x_ref[...], g_ref[...]).astype(BF16)
    z_out[...] = _dot(xn, wz_ref[...]).astype(BF16)
    xbc_out[...] = _dot(xn, wxbc_ref[...]).astype(BF16)
    dt_out[...] = _dot(xn, wdt_ref[...])
    _emit_layouts(_dot(xn, wv_ref[...]), v1_out, v4_out, v16_out, slab_ref, slab2_ref)

    cos = jnp.concatenate([cos_ref[...]] * 4, axis=1)
    sdn = jnp.concatenate([sdn_ref[...]] * 4, axis=1)
    sup = jnp.concatenate([sup_ref[...]] * 4, axis=1)

    def norm_rope(w_ref, gain_ref, scale):
        t = _dot(xn, w_ref[...])
        ssq = _dot((t * t).astype(BF16), ones_ref[...])
        t = t * lax.rsqrt(ssq * (1.0 / HEAD_DIM) + EPS) * gain_ref[...]
        half = ROPE_DIMS // 2
        t = (t * cos + pltpu.roll(t, half, 1) * sdn
             + pltpu.roll(t, ATT_WIDTH - half, 1) * sup)
        return t * scale

    _emit_layouts(norm_rope(wq_ref, qg_ref, LOG2E / math.sqrt(HEAD_DIM)), q1_out, q4_out, q16_out,
                  slab_ref, slab2_ref)
    _emit_layouts(norm_rope(wk_ref, kg_ref, 1.0), k1_out, k4_out, k16_out, slab_ref, slab2_ref)


def _inproj(x2, bsz, seq, prm, tm):
    t_tokens = x2.shape[0]
    tiles_per_seq = seq // tm
    tok = lambda w: pl.BlockSpec((tm, w), lambda i: (i, 0))
    rope = pl.BlockSpec((tm, LANES), lambda i: (i % tiles_per_seq, 0))
    cls = lambda d: pl.BlockSpec((1, d, tm // d, ATT_WIDTH),
                                 lambda i: (i // tiles_per_seq, 0, i % tiles_per_seq, 0))
    cls_shape = lambda d: jax.ShapeDtypeStruct((bsz, d, seq // d, ATT_WIDTH), BF16)
    qkv_shapes = (jax.ShapeDtypeStruct((t_tokens, ATT_WIDTH), BF16), cls_shape(4), cls_shape(16))
    qkv_specs = (tok(ATT_WIDTH), cls(4), cls(16))
    out_shape = (
        jax.ShapeDtypeStruct((t_tokens, SSD_WIDTH), BF16),
        jax.ShapeDtypeStruct((t_tokens, CONV_CH), BF16),
        jax.ShapeDtypeStruct((t_tokens, LANES), F32),
    ) + qkv_shapes * 3
    return pl.pallas_call(
        _inproj_kernel,
        grid=(t_tokens // tm,),
        in_specs=[
            tok(D_MODEL), _const_spec((1, D_MODEL)),
            _const_spec((D_MODEL, SSD_WIDTH)), _const_spec((D_MODEL, CONV_CH)),
            _const_spec((D_MODEL, LANES)), _const_spec((D_MODEL, ATT_WIDTH)),
            _const_spec((D_MODEL, ATT_WIDTH)), _const_spec((D_MODEL, ATT_WIDTH)),
            _const_spec((ATT_WIDTH, ATT_WIDTH)),
            _const_spec((1, ATT_WIDTH)), _const_spec((1, ATT_WIDTH)),
            rope, rope, rope,
        ],
        out_specs=(tok(SSD_WIDTH), tok(CONV_CH), tok(LANES)) + qkv_specs * 3,
        out_shape=out_shape,
        scratch_shapes=[pltpu.VMEM((ATT_WIDTH // LANES, tm, LANES), F32),
                        pltpu.VMEM((ATT_WIDTH // LANES, tm, LANES), F32)],
        compiler_params=_params(("parallel",)),
        name="inproj",
    )(x2, prm["mix_g"], prm["w_z"], prm["w_xbc"], prm["w_dt"], prm["w_q"], prm["w_k"], prm["w_v"],
      prm["head_ones"], prm["qg"], prm["kg"], prm["rope_cos"][seq], prm["rope_sdn"][seq],
      prm["rope_sup"][seq])


def _split3(a):
    a1 = a.astype(BF16)
    r1 = a - a1.astype(F32)
    a2 = r1.astype(BF16)
    a3 = (r1 - a2.astype(F32)).astype(BF16)
    return a1, a2, a3


def _dot_split(lhs_bf16, a):
    a1, a2, a3 = _split3(a)
    return _dot(lhs_bf16, a1) + _dot(lhs_bf16, a2) + _dot(lhs_bf16, a3)


def _ssd_kernel(nsteps, xbc_ref, xprev_ref, xnext_ref, z_ref, dt_ref, convw_ref, convb_ref, alog_ref,
                dtbias_ref, drow_ref, ng_ref, out_ref, ext_ref, carry_ref, prevb_ref, conv_ref):
    q = SSD_CHUNK
    nh = SSD_HEADS
    width = SSD_WIDTH // SSD_GROUPS
    pad = (SSD_CONV - 1) // 2
    s = pl.program_id(1)
    backward = s < nsteps
    step = jnp.where(backward, nsteps - 1 - s, s - nsteps)

    lane = _lane((q, LANES))
    row = lax.broadcasted_iota(jnp.int32, (q, LANES), 0)
    erow = lax.broadcasted_iota(jnp.int32, (LANES, 2 * SSD_WIDTH), 0)
    ecol = lax.broadcasted_iota(jnp.int32, (LANES, 2 * SSD_WIDTH), 1)
    expand = (erow == ecol // HEAD_DIM).astype(BF16)
    ltri = (lane <= row).astype(BF16)
    utri = (lane >= row).astype(BF16)

    def expand_rows(v, lo, hi):
        return _dot(v.astype(BF16), expand[:, lo:hi])

    def expand_row_exact(v, lo, hi):
        v1, v2, v3 = _split3(jnp.broadcast_to(v, (8, LANES)))
        e = expand[:, lo:hi]
        return (_dot(v1, e) + _dot(v2, e) + _dot(v3, e))[0:1]

    def step_decay(j):
        raw = dt_ref[0, j * q:(j + 1) * q, :] + dtbias_ref[...]
        softplus = jnp.maximum(raw, 0.0) + jnp.log(1.0 + jnp.exp(-jnp.abs(raw)))
        dt = jnp.where(lane < 2 * nh, softplus, 0.0)
        return dt, dt * (-jnp.exp(alog_ref[...]))

    def chunk_states(xs, bm, w_full):
        xw = (xs * w_full).astype(BF16)
        parts = []
        for g in range(SSD_GROUPS):
            bt = bm[:, g * SSD_STATE:(g + 1) * SSD_STATE].T.astype(BF16)
            parts.append(_dot(bt, xw[:, g * width:(g + 1) * width]))
        return jnp.concatenate(parts, axis=1)

    @pl.when(s == 0)
    def _():
        carry_ref[...] = jnp.zeros_like(carry_ref)

    @pl.when(s == nsteps)
    def _():
        carry_ref[...] = jnp.zeros_like(carry_ref)

    @pl.when(backward)
    def _():
        rows = SSD_CPS * q
        prev = xprev_ref[0]
        nxt = xnext_ref[0]
        ext_ref[0:BF16_ROWS, :] = jnp.where(step > 0, prev, jnp.zeros_like(prev))
        ext_ref[BF16_ROWS:BF16_ROWS + rows, :] = xbc_ref[0]
        ext_ref[BF16_ROWS + rows:, :] = jnp.where(step < nsteps - 1, nxt, jnp.zeros_like(nxt))

        win_rows = q + 2 * BF16_ROWS
        srow = lax.broadcasted_iota(jnp.int32, (q, win_rows), 0)
        scol = lax.broadcasted_iota(jnp.int32, (q, win_rows), 1)
        shifts = {k: (scol == srow + BF16_ROWS + k - pad).astype(BF16)
                  for k in range(SSD_CONV) if k != pad}

        for j in reversed(range(SSD_CPS)):
            c = step * SSD_CPS + j
            win = ext_ref[j * q:j * q + win_rows, :]
            acc = convb_ref[...] + win[BF16_ROWS:BF16_ROWS + q].astype(F32) * convw_ref[pad:pad + 1, :]
            for k, sh in shifts.items():
                acc = acc + _dot(sh, win) * convw_ref[k:k + 1, :]
            xb = _silu(acc)
            conv_ref[c] = xb.astype(BF16)
            xs = xb[:, :SSD_WIDTH]
            bm = xb[:, SSD_WIDTH:SSD_WIDTH + SSD_GROUPS * SSD_STATE]

            dt, a = step_decay(j)
            sb = _dot_split(utri, a)
            bmask = (lane >= nh) & (lane < 2 * nh)
            wb = jnp.where(bmask, jnp.exp(sb[0:1, :] - sb) * dt, 0.0)
            w_full = expand_rows(wb, SSD_WIDTH, 2 * SSD_WIDTH)
            states = chunk_states(xs, bm, w_full)
            decay = expand_row_exact(jnp.where(bmask[0:1], jnp.exp(sb[0:1, :]), 0.0),
                                     SSD_WIDTH, 2 * SSD_WIDTH)
            prevb_ref[c] = carry_ref[...].astype(BF16)
            carry_ref[...] = carry_ref[...] * decay + states

    @pl.when(jnp.logical_not(backward))
    def _():
        ii = lax.broadcasted_iota(jnp.int32, (q, q), 0)
        jj = lax.broadcasted_iota(jnp.int32, (q, q), 1)
        lane_p = _lane((q, LANES))
        for j in range(SSD_CPS):
            c = step * SSD_CPS + j
            xb16 = conv_ref[c]
            xs16 = xb16[:, :SSD_WIDTH]
            xs = xs16.astype(F32)
            bm16 = xb16[:, SSD_WIDTH:SSD_WIDTH + SSD_GROUPS * SSD_STATE]
            cm16 = xb16[:, SSD_WIDTH + SSD_GROUPS * SSD_STATE:]

            dt, a = step_decay(j)
            cf = _dot_split(ltri, a)
            sb = _dot_split(utri, a)
            pack = jnp.where(lane < nh, cf, jnp.where(lane < 2 * nh, sb, pltpu.roll(dt, 2 * nh, 1)))
            pack_t = pack.T

            cb = [_dot_nt(cm16[:, g * SSD_STATE:(g + 1) * SSD_STATE],
                          bm16[:, g * SSD_STATE:(g + 1) * SSD_STATE]) for g in range(SSD_GROUPS)]
            mats = []
            for h in range(nh):
                cf_col, cf_row = pack[:, h:h + 1], pack_t[h:h + 1, :]
                sb_col, sb_row = pack[:, nh + h:nh + h + 1], pack_t[nh + h:nh + h + 1, :]
                dtf_row = pack_t[2 * nh + h:2 * nh + h + 1, :]
                dtb_row = pack_t[3 * nh + h:3 * nh + h + 1, :]
                expo = jnp.where(jj <= ii, cf_col - cf_row, sb_col - sb_row)
                dtm = jnp.where(jj < ii, dtf_row, jnp.where(jj > ii, dtb_row, dtf_row + dtb_row))
                mats.append((cb[h // (nh // SSD_GROUPS)] * jnp.exp(expo) * dtm).astype(BF16))

            y_parts = []
            for p in range(nh // 2):
                xp = xs16[:, p * LANES:(p + 1) * LANES]
                zero = jnp.zeros_like(xp)
                rhs = jnp.concatenate([jnp.where(lane_p < HEAD_DIM, xp, zero),
                                       jnp.where(lane_p >= HEAD_DIM, xp, zero)], axis=0)
                lhs = jnp.concatenate([mats[2 * p], mats[2 * p + 1]], axis=1)
                y_parts.append(_dot(lhs, rhs))
            y = jnp.concatenate(y_parts, axis=1)

            edge = expand_rows(jnp.where(lane < 2 * nh, jnp.exp(pack), 0.0), 0, 2 * SSD_WIDTH)
            prev_f = carry_ref[...].astype(BF16)
            prev_b = prevb_ref[c]
            off_parts = []
            for g in range(SSD_GROUPS):
                st = jnp.concatenate([prev_f[:, g * width:(g + 1) * width],
                                      prev_b[:, g * width:(g + 1) * width]], axis=1)
                res = _dot(cm16[:, g * SSD_STATE:(g + 1) * SSD_STATE], st)
                off_parts.append(res[:, :width] * edge[:, g * width:(g + 1) * width]
                                 + res[:, width:] * edge[:, SSD_WIDTH + g * width:
                                                         SSD_WIDTH + (g + 1) * width])
            y = y + jnp.concatenate(off_parts, axis=1) + drow_ref[...] * xs

            y = y * _silu(z_ref[0, j * q:(j + 1) * q, :].astype(F32))
            out_ref[0, j * q:(j + 1) * q, :] = _rms(y, ng_ref[...]).astype(BF16)

            fmask = lane < nh
            wf = jnp.where(fmask, jnp.exp(cf[q - 1:q, :] - cf) * dt, 0.0)
            w_full = expand_rows(wf, 0, SSD_WIDTH)
            states = chunk_states(xs, bm16.astype(F32), w_full)
            decay = expand_row_exact(jnp.where(fmask[0:1], jnp.exp(cf[q - 1:q, :]), 0.0),
                                     0, SSD_WIDTH)
            carry_ref[...] = carry_ref[...] * decay + states


def _ssd(z, xbc, dt, prm, bsz, seq):
    q = SSD_CHUNK
    rows = SSD_CPS * q
    assert seq % rows == 0
    nsteps = seq // rows
    nc = seq // q
    halo_blocks = rows // BF16_ROWS
    conv_step = lambda s: jnp.where(s < nsteps, nsteps - 1 - s, 0)
    both_step = lambda s: jnp.where(s < nsteps, nsteps - 1 - s, s - nsteps)
    fwd_step = lambda s: jnp.maximum(s - nsteps, 0)
    z3 = z.reshape(bsz, seq, SSD_WIDTH)
    xbc3 = xbc.reshape(bsz, seq, CONV_CH)
    dt3 = dt.reshape(bsz, seq, LANES)
    in_specs = [
        pl.BlockSpec((1, rows, CONV_CH), lambda b, s: (b, conv_step(s), 0)),
        pl.BlockSpec((1, BF16_ROWS, CONV_CH),
                     lambda b, s: (b, jnp.maximum(conv_step(s) * halo_blocks - 1, 0), 0)),
        pl.BlockSpec((1, BF16_ROWS, CONV_CH),
                     lambda b, s: (b, jnp.minimum((conv_step(s) + 1) * halo_blocks,
                                                  seq // BF16_ROWS - 1), 0)),
        pl.BlockSpec((1, rows, SSD_WIDTH), lambda b, s: (b, fwd_step(s), 0)),
        pl.BlockSpec((1, rows, LANES), lambda b, s: (b, both_step(s), 0)),
        _const_spec((SSD_CONV, CONV_CH)), _const_spec((1, CONV_CH)),
        _const_spec((1, LANES)), _const_spec((1, LANES)),
        _const_spec((1, SSD_WIDTH)), _const_spec((1, SSD_WIDTH)),
    ]
    out_spec = pl.BlockSpec((1, rows, SSD_WIDTH), lambda b, s: (b, fwd_step(s), 0))
    out = pl.pallas_call(
        functools.partial(_ssd_kernel, nsteps),
        grid=(bsz, 2 * nsteps),
        in_specs=in_specs,
        out_specs=out_spec,
        out_shape=jax.ShapeDtypeStruct((bsz, seq, SSD_WIDTH), BF16),
        scratch_shapes=[
            pltpu.VMEM((rows + 2 * BF16_ROWS, CONV_CH), BF16),
            pltpu.VMEM((SSD_STATE, SSD_WIDTH), F32),
            pltpu.VMEM((nc, SSD_STATE, SSD_WIDTH), BF16),
            pltpu.VMEM((nc, q, CONV_CH), BF16),
        ],
        compiler_params=_params(("parallel", "arbitrary")),
        name="ssd",
    )(xbc3, xbc3, xbc3, z3, dt3, prm["conv_w"], prm["conv_b"], prm["alog_row"], prm["dtbias_row"],
      prm["d_row"], prm["ssd_ng"])
    return out.reshape(bsz * seq, SSD_WIDTH)


def _attn_unit(q_tile, kwin_ref, vwin_ref, row0, bias, m_old):
    low = _lane((ATT_Q, LANES)) < HEAD_DIM
    zero = jnp.zeros_like(q_tile)
    lhs = jnp.concatenate([jnp.where(low, q_tile, zero), jnp.where(low, zero, q_tile)], axis=0)
    sc = _dot_nt(lhs, kwin_ref[pl.ds(row0, ATT_KEYS), :]) + bias
    part = jnp.maximum(sc[:, :LANES], sc[:, LANES:])
    if m_old is not None:
        ninf = jnp.full_like(m_old, NEG_INF)
        part = jnp.maximum(part, jnp.concatenate([jnp.where(low, m_old, ninf),
                                                  jnp.where(low, ninf, m_old)], axis=0))
    mx = jnp.max(part, axis=-1, keepdims=True)
    pr = jnp.exp2(sc - mx).astype(BF16)
    ones = jnp.ones((ATT_KEYS, LANES), BF16)
    vcat = jnp.concatenate([vwin_ref[pl.ds(row0, ATT_KEYS), :], ones], axis=1)
    res = _dot(pr, vcat)
    num = jnp.where(low, res[:ATT_Q, :LANES], res[ATT_Q:, :LANES])
    den = jnp.where(low, res[:ATT_Q, LANES:], res[ATT_Q:, LANES:])
    mxp = jnp.where(low, jnp.broadcast_to(mx[:ATT_Q], (ATT_Q, LANES)),
                    jnp.broadcast_to(mx[ATT_Q:], (ATT_Q, LANES)))
    return num, den, mxp


def _attn_kernel(nblk, q1, k1c, k1p, k1n, v1c, v1p, v1n, q4, k4c, k4p, k4n, v4c, v4p, v4n,
                 q16, k16c, k16p, k16n, v16c, v16p, v16n, o_out, kwin, vwin, acc_ref, den_ref, max_ref,
                 bias_ref):
    n = pl.program_id(2)
    half = BAND_HALF
    rows2 = 2 * ATT_Q
    qi = lax.broadcasted_iota(jnp.int32, (rows2, ATT_KEYS), 0) % ATT_Q
    kt = lax.broadcasted_iota(jnp.int32, (rows2, ATT_KEYS), 1)
    band = jnp.where(jnp.abs(kt - half - qi) <= half, 0.0, NEG_INF)
    bias_ref[0] = band
    bias_ref[1] = jnp.where(kt < half, NEG_INF, band)
    bias_ref[2] = jnp.where(kt >= half + ATT_Q, NEG_INF, band)

    def bias_for(first_sub, last_sub):
        idx = jnp.where((n == 0) & first_sub, 1, jnp.where((n == nblk - 1) & last_sub, 2, 0))
        return bias_ref[idx]

    def fill(win, base, prev, cur, nxt, rows):
        win[pl.ds(base, half), :] = prev
        win[pl.ds(base + half, rows), :] = cur
        win[pl.ds(base + half + rows, half), :] = nxt

    def load_state(rows):
        return acc_ref[rows, :], den_ref[rows, :], max_ref[rows, :]

    def merge(old, num, den, mxp):
        scale = jnp.exp2(old[2] - mxp)
        return old[0] * scale + num, old[1] * scale + den, mxp

    def store_state(rows, new):
        acc_ref[rows, :] = new[0]
        den_ref[rows, :] = new[1]
        max_ref[rows, :] = new[2]

    span16 = ATT_KEYS

    def fill16(r, carry):
        base = pl.multiple_of(r * span16, span16)
        fill(kwin, base, k16p[0, r], k16c[0, r], k16n[0, r], ATT_Q)
        fill(vwin, base, v16p[0, r], v16c[0, r], v16n[0, r], ATT_Q)
        return carry

    lax.fori_loop(0, 16, fill16, 0)

    def group16(g, carry):
        new = []
        for u in range(ATT_GROUP):
            r = g * ATT_GROUP + u
            new.append(_attn_unit(q16[0, r], kwin, vwin, pl.multiple_of(r * span16, span16),
                                  bias_for(True, True), None))
        for u in range(ATT_GROUP):
            store_state(pl.ds(g * ATT_GROUP + u, ATT_Q, stride=16), new[u])
        return carry

    lax.fori_loop(0, 16 // ATT_GROUP, group16, 0)

    rows4 = ATT_BLK // 4
    nsub4 = rows4 // ATT_Q
    span4 = rows4 + 2 * half
    for r in range(4):
        fill(kwin, r * span4, k4p[0, r], k4c[0, r], k4n[0, r], rows4)
        fill(vwin, r * span4, v4p[0, r], v4c[0, r], v4n[0, r], rows4)

    ncls = ATT_GROUP // nsub4

    def group4(g, carry):
        units = [(g * ncls + u, j) for u in range(ncls) for j in range(nsub4)]
        rows = [pl.ds(r + 4 * ATT_Q * j, ATT_Q, stride=4) for r, j in units]
        old = [load_state(rw) for rw in rows]
        new = []
        for (r, j), prev in zip(units, old):
            unit = _attn_unit(q4[0, r, j * ATT_Q:(j + 1) * ATT_Q, :], kwin, vwin,
                              pl.multiple_of(r * span4, ATT_Q) + j * ATT_Q,
                              bias_for(j == 0, j == nsub4 - 1), prev[2])
            new.append(merge(prev, *unit))
        for rw, nw in zip(rows, new):
            store_state(rw, nw)
        return carry

    lax.fori_loop(0, 4 // ncls, group4, 0)

    fill(kwin, 0, k1p[0], k1c[0], k1n[0], ATT_BLK)
    fill(vwin, 0, v1p[0], v1c[0], v1n[0], ATT_BLK)
    nsub1 = ATT_BLK // ATT_Q

    def group1(g, carry):
        for u in range(ATT_GROUP):
            j = g * ATT_GROUP + u
            r0 = pl.multiple_of(j * ATT_Q, ATT_Q)
            rows = pl.ds(r0, ATT_Q)
            old = load_state(rows)
            unit = _attn_unit(q1[0, rows, :], kwin, vwin, r0, bias_for(j == 0, j == nsub1 - 1),
                              old[2])
            acc_new, den_new, _ = merge(old, *unit)
            o_out[0, rows, :] = (acc_new / den_new).astype(BF16)
        return carry

    lax.fori_loop(0, nsub1 // ATT_GROUP, group1, 0)


def _dilated_attention(qkv, bsz, seq):
    (q1, q4, q16), (k1, k4, k16), (v1, v4, v16) = qkv
    assert tuple(d for _, d in DILATED_PATTERNS) == (1, 4, 16)
    assert all(w // (2 * d) == BAND_HALF for w, d in DILATED_PATTERNS)
    assert seq % ATT_BLK == 0 and seq // 16 >= 2 * ATT_Q
    nblk = seq // ATT_BLK
    npair = ATT_WIDTH // LANES
    nat = lambda t: t.reshape(bsz, seq, ATT_WIDTH)

    def specs(dil):
        rows = ATT_BLK // dil
        per_blk = rows // BAND_HALF
        last = seq // dil // BAND_HALF - 1
        if dil == 1:
            cur = pl.BlockSpec((1, rows, LANES), lambda b, p, n: (b, n, p))
            prev = pl.BlockSpec((1, BAND_HALF, LANES),
                                lambda b, p, n: (b, jnp.maximum(n * per_blk - 1, 0), p))
            nxt = pl.BlockSpec((1, BAND_HALF, LANES),
                               lambda b, p, n: (b, jnp.minimum((n + 1) * per_blk, last), p))
        else:
            cur = pl.BlockSpec((1, dil, rows, LANES), lambda b, p, n: (b, 0, n, p))
            prev = pl.BlockSpec((1, dil, BAND_HALF, LANES),
                                lambda b, p, n: (b, 0, jnp.maximum(n * per_blk - 1, 0), p))
            nxt = pl.BlockSpec((1, dil, BAND_HALF, LANES),
                               lambda b, p, n: (b, 0, jnp.minimum((n + 1) * per_blk, last), p))
        return [cur, cur, prev, nxt, cur, prev, nxt]

    args = ([nat(q1)] + [nat(k1)] * 3 + [nat(v1)] * 3 + [q4] + [k4] * 3 + [v4] * 3
            + [q16] + [k16] * 3 + [v16] * 3)
    win_rows = 16 * ATT_KEYS
    out = pl.pallas_call(
        functools.partial(_attn_kernel, nblk),
        grid=(bsz, npair, nblk),
        in_specs=specs(1) + specs(4) + specs(16),
        out_specs=pl.BlockSpec((1, ATT_BLK, LANES), lambda b, p, n: (b, n, p)),
        out_shape=jax.ShapeDtypeStruct((bsz, seq, ATT_WIDTH), BF16),
        scratch_shapes=[pltpu.VMEM((win_rows, LANES), BF16),
                        pltpu.VMEM((win_rows, LANES), BF16),
                        pltpu.VMEM((ATT_BLK, LANES), F32),
                        pltpu.VMEM((ATT_BLK, LANES), F32),
                        pltpu.VMEM((ATT_BLK, LANES), F32),
                        pltpu.VMEM((3, 2 * ATT_Q, ATT_KEYS), F32)],
        compiler_params=_params(("parallel", "parallel", "parallel")),
        name="attn",
    )(*args)
    return out.reshape(bsz * seq, ATT_WIDTH)


def _memkv_kernel(mem_ref, g_ref, wkv_ref, kg_ref, kt_out, v_out):
    mn = _rms(mem_ref[0], g_ref[...]).astype(BF16)
    kv = _dot(mn, wkv_ref[...])
    scale = 1.0 / math.sqrt(XATT_HEAD_DIM)
    for h in range(XATT_HEADS):
        kh = kv[:, h * XATT_HEAD_DIM:(h + 1) * XATT_HEAD_DIM]
        kh = _rms(kh, kg_ref[...]) * scale
        kt_out[0, h * XATT_HEAD_DIM:(h + 1) * XATT_HEAD_DIM, :] = kh.T.astype(BF16)
    v_out[0] = kv[:, D_MODEL:].astype(BF16)


def _memkv(mem, prm):
    bsz = mem.shape[0]
    return pl.pallas_call(
        _memkv_kernel,
        grid=(bsz,),
        in_specs=[pl.BlockSpec((1, MEM_LEN, D_MODEL), lambda b: (b, 0, 0)),
                  _const_spec((1, D_MODEL)), _const_spec((D_MODEL, 2 * D_MODEL)),
                  _const_spec((1, XATT_HEAD_DIM))],
        out_specs=(pl.BlockSpec((1, D_MODEL, MEM_LEN), lambda b: (b, 0, 0)),
                   pl.BlockSpec((1, MEM_LEN, D_MODEL), lambda b: (b, 0, 0))),
        out_shape=(jax.ShapeDtypeStruct((bsz, D_MODEL, MEM_LEN), BF16),
                   jax.ShapeDtypeStruct((bsz, MEM_LEN, D_MODEL), BF16)),
        compiler_params=_params(("parallel",)),
        name="memkv",
    )(mem, prm["mem_g"], prm["w_kv"], prm["xkg"])


def _post_kernel(x_ref, ssd_ref, att_ref, og_ref, wo1_ref, wo2_ref, g_ref, wq_ref, qg_ref, kt_ref,
                 v_ref, wo_ref, h_out):
    att = _rms(att_ref[...].astype(F32), og_ref[...]).astype(BF16)
    h = x_ref[...] + _dot(ssd_ref[...], wo1_ref[...]) + _dot(att, wo2_ref[...])
    hn = _rms(h, g_ref[...]).astype(BF16)
    qf = _dot(hn, wq_ref[...])
    outs = []
    for hd in range(XATT_HEADS):
        sl = slice(hd * XATT_HEAD_DIM, (hd + 1) * XATT_HEAD_DIM)
        qh = _rms(qf[:, sl], qg_ref[...]).astype(BF16)
        sc = _dot(qh, kt_ref[0, sl, :])
        mx = jnp.max(sc, axis=-1, keepdims=True)
        pr = jnp.exp(sc - mx)
        den = jnp.sum(pr, axis=-1, keepdims=True)
        outs.append(_dot(pr.astype(BF16), v_ref[0, :, sl]) / den)
    o = jnp.concatenate(outs, axis=1).astype(BF16)
    h_out[...] = h + _dot(o, wo_ref[...])


def _post(x2, ssd, att, kt, vx, prm, seq, tm):
    t_tokens = x2.shape[0]
    tiles_per_seq = seq // tm
    tok = lambda w: pl.BlockSpec((tm, w), lambda i: (i, 0))
    half = D_MODEL // 2
    return pl.pallas_call(
        _post_kernel,
        grid=(t_tokens // tm,),
        in_specs=[tok(D_MODEL), tok(SSD_WIDTH), tok(ATT_WIDTH), _const_spec((1, ATT_WIDTH)),
                  _const_spec((half, D_MODEL)), _const_spec((half, D_MODEL)),
                  _const_spec((1, D_MODEL)), _const_spec((D_MODEL, D_MODEL)),
                  _const_spec((1, XATT_HEAD_DIM)),
                  pl.BlockSpec((1, D_MODEL, MEM_LEN), lambda i: (i // tiles_per_seq, 0, 0)),
                  pl.BlockSpec((1, MEM_LEN, D_MODEL), lambda i: (i // tiles_per_seq, 0, 0)),
                  _const_spec((D_MODEL, D_MODEL))],
        out_specs=tok(D_MODEL),
        out_shape=jax.ShapeDtypeStruct((t_tokens, D_MODEL), F32),
        compiler_params=_params(("parallel",)),
        name="post",
    )(x2, ssd, att, prm["att_og"], prm["w_out1"], prm["w_out2"], prm["xatt_g"], prm["w_xq"],
      prm["xqg"], kt, vx, prm["w_xo"])


def _mlp_kernel(ff_chunk, h_ref, g_ref, w1_ref, w2_ref, y_out):
    h = h_ref[...]
    hm = _rms(h, g_ref[...]).astype(BF16)
    acc = h
    for c in range(D_FF // ff_chunk):
        sl = slice(c * ff_chunk, (c + 1) * ff_chunk)
        a = jnp.maximum(_dot(hm, w1_ref[:, sl]), 0.0)
        acc = acc + _dot((a * a).astype(BF16), w2_ref[sl, :])
    y_out[...] = acc


def _mlp(h, prm, tm):
    t_tokens = h.shape[0]
    tok = pl.BlockSpec((tm, D_MODEL), lambda i: (i, 0))
    return pl.pallas_call(
        functools.partial(_mlp_kernel, 1024),
        grid=(t_tokens // tm,),
        in_specs=[tok, _const_spec((1, D_MODEL)), _const_spec((D_MODEL, D_FF)),
                  _const_spec((D_FF, D_MODEL))],
        out_specs=tok,
        out_shape=jax.ShapeDtypeStruct((t_tokens, D_MODEL), F32),
        compiler_params=_params(("parallel",)),
        name="mlp",
    )(h, prm["mlp_g"], prm["w1"], prm["w2"])


def _rope_tables(seq):
    half = ROPE_DIMS // 2
    inv_freq = jnp.power(jnp.float32(ROPE_THETA), -jnp.arange(half, dtype=F32) / half)
    ang = jnp.arange(seq).astype(F32)[:, None] * inv_freq[None, :]
    cos, sin = jnp.cos(ang), jnp.sin(ang)
    zeros = jnp.zeros((seq, half), F32)
    rest = HEAD_DIM - ROPE_DIMS
    one_head = lambda a, b, fill: jnp.concatenate([a, b, jnp.full((seq, rest), fill, F32)], axis=1)
    two = lambda t: jnp.concatenate([t, t], axis=1)
    return (two(one_head(cos, cos, 1.0)), two(one_head(zeros, sin, 0.0)),
            two(one_head(-sin, zeros, 0.0)))


def _prepare(seqs, mix_norm_g, w_in, conv_w, conv_b, ssd_A_log, ssd_dt_bias, ssd_D, ssd_norm_g,
             att_q_norm_g, att_k_norm_g, att_out_norm_g, w_out, xatt_norm_g, mem_norm_g, xatt_wq,
             xatt_wkv, xatt_q_norm_g, xatt_k_norm_g, xatt_wo, mlp_norm_g, mlp_w1, mlp_w2):
    row = lambda t: t.reshape(1, -1).astype(F32)
    pad_lanes = lambda t: jnp.pad(t, ((0, 0), (0, LANES - t.shape[1])))
    head_id = jnp.arange(ATT_WIDTH) // HEAD_DIM
    prm = {
        "mix_g": row(mix_norm_g),
        "w_z": w_in[:, OFF_Z:OFF_XBC].astype(BF16),
        "w_xbc": w_in[:, OFF_XBC:OFF_DT].astype(BF16),
        "w_dt": pad_lanes(w_in[:, OFF_DT:OFF_Q]).astype(BF16),
        "w_q": w_in[:, OFF_Q:OFF_K].astype(BF16),
        "w_k": w_in[:, OFF_K:OFF_V].astype(BF16),
        "w_v": w_in[:, OFF_V:IN_COLS].astype(BF16),
        "head_ones": (head_id[:, None] == head_id[None, :]).astype(BF16),
        "qg": jnp.tile(row(att_q_norm_g), (1, ATT_HEADS)),
        "kg": jnp.tile(row(att_k_norm_g), (1, ATT_HEADS)),
        "conv_w": conv_w.astype(F32),
        "conv_b": row(conv_b),
        "alog_row": pad_lanes(row(ssd_A_log)),
        "dtbias_row": pad_lanes(row(ssd_dt_bias)),
        "d_row": jnp.repeat(row(ssd_D), HEAD_DIM, axis=1),
        "ssd_ng": row(ssd_norm_g),
        "att_og": row(att_out_norm_g),
        "w_out1": w_out[:SSD_WIDTH].astype(BF16),
        "w_out2": w_out[SSD_WIDTH:].astype(BF16),
        "xatt_g": row(xatt_norm_g),
        "mem_g": row(mem_norm_g),
        "w_xq": xatt_wq.astype(BF16),
        "w_kv": xatt_wkv.astype(BF16),
        "xqg": row(xatt_q_norm_g),
        "xkg": row(xatt_k_norm_g),
        "w_xo": xatt_wo.astype(BF16),
        "mlp_g": row(mlp_norm_g),
        "w1": mlp_w1.astype(BF16),
        "w2": mlp_w2.astype(BF16),
        "rope_cos": {}, "rope_sdn": {}, "rope_sup": {},
    }
    for seq in set(seqs):
        cos, sdn, sup = _rope_tables(seq)
        prm["rope_cos"][seq], prm["rope_sdn"][seq], prm["rope_sup"][seq] = cos, sdn, sup
    return prm


def _token_tile(seq):
    return min(512, seq)


def _layer(x, mem, prm):
    bsz, seq, _ = x.shape
    tm = _token_tile(seq)
    x2 = x.reshape(bsz * seq, D_MODEL)
    outs = _inproj(x2, bsz, seq, prm, tm)
    z, xbc, dt = outs[:3]
    ssd = _ssd(z, xbc, dt, prm, bsz, seq)
    att = _dilated_attention((outs[3:6], outs[6:9], outs[9:12]), bsz, seq)
    kt, vx = _memkv(mem, prm)
    h = _post(x2, ssd, att, kt, vx, prm, seq, tm)
    y = _mlp(h, prm, tm)
    return y.reshape(bsz, seq, D_MODEL)


def kernel(x_prompt, x_sample, mem_prompt, mem_sample, mix_norm_g, w_in, conv_w, conv_b, ssd_A_log,
           ssd_dt_bias, ssd_D, ssd_norm_g, att_q_norm_g, att_k_norm_g, att_out_norm_g, w_out,
           xatt_norm_g, mem_norm_g, xatt_wq, xatt_wkv, xatt_q_norm_g, xatt_k_norm_g, xatt_wo,
           mlp_norm_g, mlp_w1, mlp_w2):
    weights = (mix_norm_g, w_in, conv_w, conv_b, ssd_A_log, ssd_dt_bias, ssd_D, ssd_norm_g,
               att_q_norm_g, att_k_norm_g, att_out_norm_g, w_out, xatt_norm_g, mem_norm_g, xatt_wq,
               xatt_wkv, xatt_q_norm_g, xatt_k_norm_g, xatt_wo, mlp_norm_g, mlp_w1, mlp_w2)
    assert all(w.shape[0] == 1 for w in weights), "single-layer stack expected"
    prm = _prepare((x_prompt.shape[1], x_sample.shape[1]), *(w[0] for w in weights))
    return (_layer(x_prompt, mem_prompt, prm), _layer(x_sample, mem_sample, prm))
```

```python
import functools
import math

import jax
import jax.numpy as jnp
from jax import lax
from jax.experimental import pallas as pl
from jax.experimental.pallas import tpu as pltpu

F32 = jnp.float32
BF16 = jnp.bfloat16

D_MODEL = 1024
HEAD_DIM = 64
SSD_HEADS = 8
SSD_WIDTH = SSD_HEADS * HEAD_DIM
SSD_GROUPS = 2
SSD_STATE = 128
SSD_CONV = 5
SSD_CHUNK = 128
ATT_HEADS = 8
ATT_WIDTH = ATT_HEADS * HEAD_DIM
DILATED_PATTERNS = ((128, 1), (512, 4), (2048, 16))
ROPE_DIMS = HEAD_DIM // 4
ROPE_THETA = 500000.0
MEM_LEN = 256
XATT_HEADS = 4
XATT_HEAD_DIM = D_MODEL // XATT_HEADS
D_FF = 4 * D_MODEL
EPS = 1e-6
NEG_INF = -1e30
CONV_CH = SSD_WIDTH + 2 * SSD_GROUPS * SSD_STATE
OFF_Z = 0
OFF_XBC = OFF_Z + SSD_WIDTH
OFF_DT = OFF_XBC + CONV_CH
OFF_Q = OFF_DT + 2 * SSD_HEADS
OFF_K = OFF_Q + ATT_WIDTH
OFF_V = OFF_K + ATT_WIDTH
IN_COLS = OFF_V + ATT_WIDTH

LANES = 128
BF16_ROWS = 16
VMEM_LIMIT = 56 * 1024 * 1024
SSD_CPS = 4
BAND_HALF = 64
ATT_Q = 128
ATT_KEYS = ATT_Q + 2 * BAND_HALF
ATT_BLK = 2048
ATT_GROUP = 16
LOG2E = math.log2(math.e)


def _dot(a, b):
    return jnp.dot(a, b, preferred_element_type=F32)


def _dot_nt(a, b):
    return lax.dot_general(a, b, (((1,), (1,)), ((), ())), preferred_element_type=F32)


def _const_spec(shape):
    zeros = (0,) * len(shape)
    return pl.BlockSpec(shape, lambda *_: zeros, pipeline_mode=pl.Buffered(1))


def _params(sem):
    return pltpu.CompilerParams(dimension_semantics=sem, vmem_limit_bytes=VMEM_LIMIT)


def _lane(shape):
    return lax.broadcasted_iota(jnp.int32, shape, len(shape) - 1)


def _rms(x, g):
    ms = jnp.mean(x * x, axis=-1, keepdims=True)
    return x * lax.rsqrt(ms + EPS) * g


def _silu(x):
    h = 0.5 * x
    return h + h * jnp.tanh(h)


def _emit_layouts(t, nat_out, c4_out, c16_out, slab_ref, slab2_ref):
    tm = t.shape[0]
    n4, n16 = tm // 4, tm // 16
    nat_out[...] = t.astype(BF16)
    for p in range(ATT_WIDTH // LANES):
        sl = slice(p * LANES, (p + 1) * LANES)
        slab_ref[p] = t[:, sl]
        for r4 in range(4):
            c4 = slab_ref[p, pl.ds(r4, n4, stride=4), :]
            c4_out[0, r4, :, sl] = c4.astype(BF16)
            slab2_ref[p, r4 * n4:(r4 + 1) * n4, :] = c4
        for r4 in range(4):
            for a in range(4):
                c16 = slab2_ref[p, pl.ds(r4 * n4 + a, n16, stride=4), :]
                c16_out[0, 4 * a + r4, :, sl] = c16.astype(BF16)


def _inproj_kernel(x_ref, g_ref, wz_ref, wxbc_ref, wdt_ref, wq_ref, wk_ref, wv_ref, ones_ref,
                   qg_ref, kg_ref, cos_ref, sdn_ref, sup_ref,
                   z_out, xbc_out, dt_out, q1_out, q4_out, q16_out, k1_out, k4_out, k16_out,
                   v1_out, v4_out, v16_out, slab_ref, slab2_ref):
    xn = _rms(x_ref[...], g_ref[...]).astype(BF16)

    cos = jnp.concatenate([cos_ref[...]] * 4, axis=1)
    sdn = jnp.concatenate([sdn_ref[...]] * 4, axis=1)
    sup = jnp.concatenate([sup_ref[...]] * 4, axis=1)

    def norm_rope(w_ref, gain_ref, scale):
        t = _dot(xn, w_ref[...])
        ssq = _dot((t * t).astype(BF16), ones_ref[...])
        t = t * lax.rsqrt(ssq * (1.0 / HEAD_DIM) + EPS) * gain_ref[...]
        half = ROPE_DIMS // 2
        t = (t * cos + pltpu.roll(t, half, 1) * sdn
             + pltpu.roll(t, ATT_WIDTH - half, 1) * sup)
        return t * scale

    _emit_layouts(norm_rope(wq_ref, qg_ref, LOG2E / math.sqrt(HEAD_DIM)), q1_out, q4_out, q16_out,
                  slab_ref.at[0], slab2_ref.at[0])
    z_out[...] = _dot(xn, wz_ref[...]).astype(BF16)
    _emit_layouts(norm_rope(wk_ref, kg_ref, 1.0), k1_out, k4_out, k16_out,
                  slab_ref.at[1], slab2_ref.at[1])
    xbc_out[...] = _dot(xn, wxbc_ref[...]).astype(BF16)
    _emit_layouts(_dot(xn, wv_ref[...]), v1_out, v4_out, v16_out, slab_ref.at[2], slab2_ref.at[2])
    dt_out[...] = _dot(xn, wdt_ref[...])


def _inproj(x2, bsz, seq, prm, tm):
    t_tokens = x2.shape[0]
    tiles_per_seq = seq // tm
    tok = lambda w: pl.BlockSpec((tm, w), lambda i: (i, 0))
    rope = pl.BlockSpec((tm, LANES), lambda i: (i % tiles_per_seq, 0))
    cls = lambda d: pl.BlockSpec((1, d, tm // d, ATT_WIDTH),
                                 lambda i: (i // tiles_per_seq, 0, i % tiles_per_seq, 0))
    cls_shape = lambda d: jax.ShapeDtypeStruct((bsz, d, seq // d, ATT_WIDTH), BF16)
    qkv_shapes = (jax.ShapeDtypeStruct((t_tokens, ATT_WIDTH), BF16), cls_shape(4), cls_shape(16))
    qkv_specs = (tok(ATT_WIDTH), cls(4), cls(16))
    out_shape = (
        jax.ShapeDtypeStruct((t_tokens, SSD_WIDTH), BF16),
        jax.ShapeDtypeStruct((t_tokens, CONV_CH), BF16),
        jax.ShapeDtypeStruct((t_tokens, LANES), F32),
    ) + qkv_shapes * 3
    return pl.pallas_call(
        _inproj_kernel,
        grid=(t_tokens // tm,),
        in_specs=[
            tok(D_MODEL), _const_spec((1, D_MODEL)),
            _const_spec((D_MODEL, SSD_WIDTH)), _const_spec((D_MODEL, CONV_CH)),
            _const_spec((D_MODEL, LANES)), _const_spec((D_MODEL, ATT_WIDTH)),
            _const_spec((D_MODEL, ATT_WIDTH)), _const_spec((D_MODEL, ATT_WIDTH)),
            _const_spec((ATT_WIDTH, ATT_WIDTH)),
            _const_spec((1, ATT_WIDTH)), _const_spec((1, ATT_WIDTH)),
            rope, rope, rope,
        ],
        out_specs=(tok(SSD_WIDTH), tok(CONV_CH), tok(LANES)) + qkv_specs * 3,
        out_shape=out_shape,
        scratch_shapes=[pltpu.VMEM((3, ATT_WIDTH // LANES, tm, LANES), F32),
                        pltpu.VMEM((3, ATT_WIDTH // LANES, tm, LANES), F32)],
        compiler_params=_params(("parallel",)),
        name="inproj",
    )(x2, prm["mix_g"], prm["w_z"], prm["w_xbc"], prm["w_dt"], prm["w_q"], prm["w_k"], prm["w_v"],
      prm["head_ones"], prm["qg"], prm["kg"], prm["rope_cos"][seq], prm["rope_sdn"][seq],
      prm["rope_sup"][seq])


def _split3(a):
    a1 = a.astype(BF16)
    r1 = a - a1.astype(F32)
    a2 = r1.astype(BF16)
    a3 = (r1 - a2.astype(F32)).astype(BF16)
    return a1, a2, a3


def _dot_split(lhs_bf16, a):
    a1, a2, a3 = _split3(a)
    return _dot(lhs_bf16, a1) + _dot(lhs_bf16, a2) + _dot(lhs_bf16, a3)


def _ssd_kernel(nsteps, xbc_ref, xprev_ref, xnext_ref, z_ref, dt_ref, convw_ref, convb_ref, alog_ref,
                dtbias_ref, drow_ref, ng_ref, out_ref, ext_ref, carry_ref, prevb_ref, conv_ref):
    q = SSD_CHUNK
    nh = SSD_HEADS
    width = SSD_WIDTH // SSD_GROUPS
    pad = (SSD_CONV - 1) // 2
    s = pl.program_id(1)
    backward = s < nsteps
    step = jnp.where(backward, nsteps - 1 - s, s - nsteps)

    lane = _lane((q, LANES))
    row = lax.broadcasted_iota(jnp.int32, (q, LANES), 0)
    erow = lax.broadcasted_iota(jnp.int32, (LANES, 2 * SSD_WIDTH), 0)
    ecol = lax.broadcasted_iota(jnp.int32, (LANES, 2 * SSD_WIDTH), 1)
    expand = (erow == ecol // HEAD_DIM).astype(BF16)
    ltri = (lane <= row).astype(BF16)
    utri = (lane >= row).astype(BF16)

    def expand_rows(v, lo, hi):
        return _dot(v.astype(BF16), expand[:, lo:hi])

    def expand_row_exact(v, lo, hi):
        v1, v2, v3 = _split3(jnp.broadcast_to(v, (8, LANES)))
        e = expand[:, lo:hi]
        return (_dot(v1, e) + _dot(v2, e) + _dot(v3, e))[0:1]

    def step_decay(j):
        raw = dt_ref[0, j * q:(j + 1) * q, :] + dtbias_ref[...]
        softplus = jnp.maximum(raw, 0.0) + jnp.log(1.0 + jnp.exp(-jnp.abs(raw)))
        dt = jnp.where(lane < 2 * nh, softplus, 0.0)
        return dt, dt * (-jnp.exp(alog_ref[...]))

    def chunk_states(xs, bm16, w_full):
        xw = (xs * w_full).astype(BF16)
        parts = []
        for g in range(SSD_GROUPS):
            bt = bm16[:, g * SSD_STATE:(g + 1) * SSD_STATE].astype(F32).T.astype(BF16)
            parts.append(_dot(bt, xw[:, g * width:(g + 1) * width]))
        return jnp.concatenate(parts, axis=1)

    @pl.when(s == 0)
    def _():
        carry_ref[...] = jnp.zeros_like(carry_ref)

    @pl.when(s == nsteps)
    def _():
        carry_ref[...] = jnp.zeros_like(carry_ref)

    @pl.when(backward)
    def _():
        rows = SSD_CPS * q
        prev = xprev_ref[0]
        nxt = xnext_ref[0]
        ext_ref[0:BF16_ROWS, :] = jnp.where(step > 0, prev, jnp.zeros_like(prev))
        ext_ref[BF16_ROWS:BF16_ROWS + rows, :] = xbc_ref[0]
        ext_ref[BF16_ROWS + rows:, :] = jnp.where(step < nsteps - 1, nxt, jnp.zeros_like(nxt))

        win_rows = q + 2 * BF16_ROWS
        srow = lax.broadcasted_iota(jnp.int32, (q, win_rows), 0)
        scol = lax.broadcasted_iota(jnp.int32, (q, win_rows), 1)
        shifts = {k: (scol == srow + BF16_ROWS + k - pad).astype(BF16)
                  for k in range(SSD_CONV) if k != pad}

        bmask = (lane >= nh) & (lane < 2 * nh)

        def decay_terms(j):
            dt, a = step_decay(j)
            sb = _dot_split(utri, a)
            wb = jnp.where(bmask, jnp.exp(sb[0:1, :] - sb) * dt, 0.0)
            decay = expand_row_exact(jnp.where(bmask[0:1], jnp.exp(sb[0:1, :]), 0.0),
                                     SSD_WIDTH, 2 * SSD_WIDTH)
            return expand_rows(wb, SSD_WIDTH, 2 * SSD_WIDTH), decay

        def conv_dots(j):
            win = ext_ref[j * q:j * q + win_rows, :]
            return win, {k: _dot(sh, win) for k, sh in shifts.items()}

        def conv_finish(j, win, dots):
            acc = convb_ref[...] + win[BF16_ROWS:BF16_ROWS + q].astype(F32) * convw_ref[pad:pad + 1, :]
            for k, d in dots.items():
                acc = acc + d * convw_ref[k:k + 1, :]
            conv_ref[step * SSD_CPS + j] = _silu(acc).astype(BF16)

        pending = conv_dots(0)
        terms = []
        for j in range(SSD_CPS):
            ahead = conv_dots(j + 1) if j + 1 < SSD_CPS else None
            terms.append(decay_terms(j))
            conv_finish(j, *pending)
            pending = ahead

        states = []
        for j in range(SSD_CPS):
            xb16 = conv_ref[step * SSD_CPS + j]
            states.append(chunk_states(xb16[:, :SSD_WIDTH].astype(F32),
                                       xb16[:, SSD_WIDTH:SSD_WIDTH + SSD_GROUPS * SSD_STATE],
                                       terms[j][0]))
        carry = carry_ref[...]
        for j in reversed(range(SSD_CPS)):
            prevb_ref[step * SSD_CPS + j] = carry.astype(BF16)
            carry = carry * terms[j][1] + states[j]
        carry_ref[...] = carry

    @pl.when(jnp.logical_not(backward))
    def _():
        ii = lax.broadcasted_iota(jnp.int32, (q, q), 0)
        jj = lax.broadcasted_iota(jnp.int32, (q, q), 1)
        lane_p = _lane((q, LANES))
        fmask = lane < nh
        pre = []
        prevs = []
        carry = carry_ref[...]

        def decay_terms(j):
            dt, a = step_decay(j)
            cf = _dot_split(ltri, a)
            sb = _dot_split(utri, a)
            ldt = jnp.where(lane < 2 * nh, jnp.log2(dt), 0.0)
            pack = jnp.where(fmask, cf * LOG2E,
                             jnp.where(lane < 2 * nh, sb * LOG2E, pltpu.roll(ldt, 2 * nh, 1)))
            pack_t = pack.T
            dsum = dt + pltpu.roll(dt, LANES - nh, 1)
            dsum_t = jnp.where(fmask, jnp.log2(dsum), 0.0).T
            edge = expand_rows(jnp.where(lane < 2 * nh, jnp.exp2(pack), 0.0), 0, 2 * SSD_WIDTH)
            wf = jnp.where(fmask, jnp.exp(cf[q - 1:q, :] - cf) * dt, 0.0)
            decay = expand_row_exact(jnp.where(fmask[0:1], jnp.exp(cf[q - 1:q, :]), 0.0),
                                     0, SSD_WIDTH)
            pre.append((pack, pack_t, dsum_t, edge, expand_rows(wf, 0, SSD_WIDTH), decay))

        for j in range(SSD_CPS):
            decay_terms(j)
        for j in range(SSD_CPS):
            xb16 = conv_ref[step * SSD_CPS + j]
            st = chunk_states(xb16[:, :SSD_WIDTH].astype(F32),
                              xb16[:, SSD_WIDTH:SSD_WIDTH + SSD_GROUPS * SSD_STATE], pre[j][4])
            prevs.append(carry.astype(BF16))
            carry = carry * pre[j][5] + st
        carry_ref[...] = carry

        for j in range(SSD_CPS):
            c = step * SSD_CPS + j
            pack, pack_t, dsum_t, edge = pre[j][:4]
            xb16 = conv_ref[c]
            xs16 = xb16[:, :SSD_WIDTH]
            xs = xs16.astype(F32)
            bm16 = xb16[:, SSD_WIDTH:SSD_WIDTH + SSD_GROUPS * SSD_STATE]
            cm16 = xb16[:, SSD_WIDTH + SSD_GROUPS * SSD_STATE:]

            cb = [_dot_nt(cm16[:, g * SSD_STATE:(g + 1) * SSD_STATE],
                          bm16[:, g * SSD_STATE:(g + 1) * SSD_STATE]) for g in range(SSD_GROUPS)]
            mats = []
            for h in range(nh):
                low_row = pack_t[h:h + 1, :] - pack_t[2 * nh + h:2 * nh + h + 1, :]
                up_row = pack_t[nh + h:nh + h + 1, :] - pack_t[3 * nh + h:3 * nh + h + 1, :]
                expo = jnp.where(jj < ii, pack[:, h:h + 1] - low_row,
                                 jnp.where(jj > ii, pack[:, nh + h:nh + h + 1] - up_row,
                                           dsum_t[h:h + 1, :]))
                mats.append((cb[h // (nh // SSD_GROUPS)] * jnp.exp2(expo)).astype(BF16))

            y_parts = []
            for p in range(nh // 2):
                xp = xs16[:, p * LANES:(p + 1) * LANES]
                zero = jnp.zeros_like(xp)
                rhs = jnp.concatenate([jnp.where(lane_p < HEAD_DIM, xp, zero),
                                       jnp.where(lane_p >= HEAD_DIM, xp, zero)], axis=0)
                lhs = jnp.concatenate([mats[2 * p], mats[2 * p + 1]], axis=1)
                y_parts.append(_dot(lhs, rhs))
            y = jnp.concatenate(y_parts, axis=1)

            prev_f = prevs[j]
            prev_b = prevb_ref[c]
            off_parts = []
            for g in range(SSD_GROUPS):
                st = jnp.concatenate([prev_f[:, g * width:(g + 1) * width],
                                      prev_b[:, g * width:(g + 1) * width]], axis=1)
                res = _dot(cm16[:, g * SSD_STATE:(g + 1) * SSD_STATE], st)
                off_parts.append(res[:, :width] * edge[:, g * width:(g + 1) * width]
                                 + res[:, width:] * edge[:, SSD_WIDTH + g * width:
                                                         SSD_WIDTH + (g + 1) * width])
            y = y + jnp.concatenate(off_parts, axis=1) + drow_ref[...] * xs

            y = y * _silu(z_ref[0, j * q:(j + 1) * q, :].astype(F32))
            out_ref[0, j * q:(j + 1) * q, :] = _rms(y, ng_ref[...]).astype(BF16)


def _ssd(z, xbc, dt, prm, bsz, seq):
    q = SSD_CHUNK
    rows = SSD_CPS * q
    assert seq % rows == 0
    nsteps = seq // rows
    nc = seq // q
    halo_blocks = rows // BF16_ROWS
    conv_step = lambda s: jnp.where(s < nsteps, nsteps - 1 - s, 0)
    both_step = lambda s: jnp.where(s < nsteps, nsteps - 1 - s, s - nsteps)
    fwd_step = lambda s: jnp.maximum(s - nsteps, 0)
    z3 = z.reshape(bsz, seq, SSD_WIDTH)
    xbc3 = xbc.reshape(bsz, seq, CONV_CH)
    dt3 = dt.reshape(bsz, seq, LANES)
    in_specs = [
        pl.BlockSpec((1, rows, CONV_CH), lambda b, s: (b, conv_step(s), 0)),
        pl.BlockSpec((1, BF16_ROWS, CONV_CH),
                     lambda b, s: (b, jnp.maximum(conv_step(s) * halo_blocks - 1, 0), 0)),
        pl.BlockSpec((1, BF16_ROWS, CONV_CH),
                     lambda b, s: (b, jnp.minimum((conv_step(s) + 1) * halo_blocks,
                                                  seq // BF16_ROWS - 1), 0)),
        pl.BlockSpec((1, rows, SSD_WIDTH), lambda b, s: (b, fwd_step(s), 0)),
        pl.BlockSpec((1, rows, LANES), lambda b, s: (b, both_step(s), 0)),
        _const_spec((SSD_CONV, CONV_CH)), _const_spec((1, CONV_CH)),
        _const_spec((1, LANES)), _const_spec((1, LANES)),
        _const_spec((1, SSD_WIDTH)), _const_spec((1, SSD_WIDTH)),
    ]
    out_spec = pl.BlockSpec((1, rows, SSD_WIDTH), lambda b, s: (b, fwd_step(s), 0))
    out = pl.pallas_call(
        functools.partial(_ssd_kernel, nsteps),
        grid=(bsz, 2 * nsteps),
        in_specs=in_specs,
        out_specs=out_spec,
        out_shape=jax.ShapeDtypeStruct((bsz, seq, SSD_WIDTH), BF16),
        scratch_shapes=[
            pltpu.VMEM((rows + 2 * BF16_ROWS, CONV_CH), BF16),
            pltpu.VMEM((SSD_STATE, SSD_WIDTH), F32),
            pltpu.VMEM((nc, SSD_STATE, SSD_WIDTH), BF16),
            pltpu.VMEM((nc, q, CONV_CH), BF16),
        ],
        compiler_params=_params(("parallel", "arbitrary")),
        name="ssd",
    )(xbc3, xbc3, xbc3, z3, dt3, prm["conv_w"], prm["conv_b"], prm["alog_row"], prm["dtbias_row"],
      prm["d_row"], prm["ssd_ng"])
    return out.reshape(bsz * seq, SSD_WIDTH)


def _attn_unit(q_tile, kwin_ref, vwin_ref, row0, bias, m_old):
    low = _lane((ATT_Q, LANES)) < HEAD_DIM
    zero = jnp.zeros_like(q_tile)
    lhs = jnp.concatenate([jnp.where(low, q_tile, zero), jnp.where(low, zero, q_tile)], axis=0)
    sc = _dot_nt(lhs, kwin_ref[pl.ds(row0, ATT_KEYS), :]) + bias
    part = jnp.maximum(sc[:, :LANES], sc[:, LANES:])
    if m_old is not None:
        ninf = jnp.full_like(m_old, NEG_INF)
        part = jnp.maximum(part, jnp.concatenate([jnp.where(low, m_old, ninf),
                                                  jnp.where(low, ninf, m_old)], axis=0))
    mx = jnp.max(part, axis=-1, keepdims=True)
    pr = jnp.exp2(sc - mx).astype(BF16)
    ones = jnp.ones((ATT_KEYS, LANES), BF16)
    vcat = jnp.concatenate([vwin_ref[pl.ds(row0, ATT_KEYS), :], ones], axis=1)
    res = _dot(pr, vcat)
    num = jnp.where(low, res[:ATT_Q, :LANES], res[ATT_Q:, :LANES])
    den = jnp.where(low, res[:ATT_Q, LANES:], res[ATT_Q:, LANES:])
    mxp = jnp.where(low, jnp.broadcast_to(mx[:ATT_Q], (ATT_Q, LANES)),
                    jnp.broadcast_to(mx[ATT_Q:], (ATT_Q, LANES)))
    return num, den, mxp


def _attn_kernel(nblk, q1, k1c, k1p, k1n, v1c, v1p, v1n, q4, k4c, k4p, k4n, v4c, v4p, v4n,
                 q16, k16c, k16p, k16n, v16c, v16p, v16n, o_out, kwin, vwin, acc_ref, den_ref, max_ref,
                 bias_ref):
    n = pl.program_id(2)
    half = BAND_HALF
    rows2 = 2 * ATT_Q
    qi = lax.broadcasted_iota(jnp.int32, (rows2, ATT_KEYS), 0) % ATT_Q
    kt = lax.broadcasted_iota(jnp.int32, (rows2, ATT_KEYS), 1)
    band = jnp.where(jnp.abs(kt - half - qi) <= half, 0.0, NEG_INF)
    bias_ref[0] = band
    bias_ref[1] = jnp.where(kt < half, NEG_INF, band)
    bias_ref[2] = jnp.where(kt >= half + ATT_Q, NEG_INF, band)

    def bias_for(first_sub, last_sub):
        idx = jnp.where((n == 0) & first_sub, 1, jnp.where((n == nblk - 1) & last_sub, 2, 0))
        return bias_ref[idx]

    def fill(win, base, prev, cur, nxt, rows):
        win[pl.ds(base, half), :] = prev
        win[pl.ds(base + half, rows), :] = cur
        win[pl.ds(base + half + rows, half), :] = nxt

    def load_state(rows):
        return acc_ref[rows, :], den_ref[rows, :], max_ref[rows, :]

    def merge(old, num, den, mxp):
        scale = jnp.exp2(old[2] - mxp)
        return old[0] * scale + num, old[1] * scale + den, mxp

    def store_state(rows, new):
        acc_ref[rows, :] = new[0]
        den_ref[rows, :] = new[1]
        max_ref[rows, :] = new[2]

    span16 = ATT_KEYS

    def fill16(r, carry):
        base = pl.multiple_of(r * span16, span16)
        fill(kwin, base, k16p[0, r], k16c[0, r], k16n[0, r], ATT_Q)
        fill(vwin, base, v16p[0, r], v16c[0, r], v16n[0, r], ATT_Q)
        return carry

    lax.fori_loop(0, 16, fill16, 0)

    def group16(g, carry):
        new = []
        for u in range(ATT_GROUP):
            r = g * ATT_GROUP + u
            new.append(_attn_unit(q16[0, r], kwin, vwin, pl.multiple_of(r * span16, span16),
                                  bias_for(True, True), None))
        for u in range(ATT_GROUP):
            store_state(pl.ds(g * ATT_GROUP + u, ATT_Q, stride=16), new[u])
        return carry

    lax.fori_loop(0, 16 // ATT_GROUP, group16, 0)

    rows4 = ATT_BLK // 4
    nsub4 = rows4 // ATT_Q
    span4 = rows4 + 2 * half
    for r in range(4):
        fill(kwin, r * span4, k4p[0, r], k4c[0, r], k4n[0, r], rows4)
        fill(vwin, r * span4, v4p[0, r], v4c[0, r], v4n[0, r], rows4)

    ncls = ATT_GROUP // nsub4

    def group4(g, carry):
        units = [(g * ncls + u, j) for u in range(ncls) for j in range(nsub4)]
        rows = [pl.ds(r + 4 * ATT_Q * j, ATT_Q, stride=4) for r, j in units]
        old = [load_state(rw) for rw in rows]
        new = []
        for (r, j), prev in zip(units, old):
            unit = _attn_unit(q4[0, r, j * ATT_Q:(j + 1) * ATT_Q, :], kwin, vwin,
                              pl.multiple_of(r * span4, ATT_Q) + j * ATT_Q,
                              bias_for(j == 0, j == nsub4 - 1), prev[2])
            new.append(merge(prev, *unit))
        for rw, nw in zip(rows, new):
            store_state(rw, nw)
        return carry

    lax.fori_loop(0, 4 // ncls, group4, 0)

    fill(kwin, 0, k1p[0], k1c[0], k1n[0], ATT_BLK)
    fill(vwin, 0, v1p[0], v1c[0], v1n[0], ATT_BLK)
    nsub1 = ATT_BLK // ATT_Q

    def group1(g, carry):
        for u in range(ATT_GROUP):
            j = g * ATT_GROUP + u
            r0 = pl.multiple_of(j * ATT_Q, ATT_Q)
            rows = pl.ds(r0, ATT_Q)
            old = load_state(rows)
            unit = _attn_unit(q1[0, rows, :], kwin, vwin, r0, bias_for(j == 0, j == nsub1 - 1),
                              old[2])
            acc_new, den_new, _ = merge(old, *unit)
            o_out[0, rows, :] = (acc_new / den_new).astype(BF16)
        return carry

    lax.fori_loop(0, nsub1 // ATT_GROUP, group1, 0)


def _dilated_attention(qkv, bsz, seq):
    (q1, q4, q16), (k1, k4, k16), (v1, v4, v16) = qkv
    assert tuple(d for _, d in DILATED_PATTERNS) == (1, 4, 16)
    assert all(w // (2 * d) == BAND_HALF for w, d in DILATED_PATTERNS)
    assert seq % ATT_BLK == 0 and seq // 16 >= 2 * ATT_Q
    nblk = seq // ATT_BLK
    npair = ATT_WIDTH // LANES
    nat = lambda t: t.reshape(bsz, seq, ATT_WIDTH)

    def specs(dil):
        rows = ATT_BLK // dil
        per_blk = rows // BAND_HALF
        last = seq // dil // BAND_HALF - 1
        if dil == 1:
            cur = pl.BlockSpec((1, rows, LANES), lambda b, p, n: (b, n, p))
            prev = pl.BlockSpec((1, BAND_HALF, LANES),
                                lambda b, p, n: (b, jnp.maximum(n * per_blk - 1, 0), p))
            nxt = pl.BlockSpec((1, BAND_HALF, LANES),
                               lambda b, p, n: (b, jnp.minimum((n + 1) * per_blk, last), p))
        else:
            cur = pl.BlockSpec((1, dil, rows, LANES), lambda b, p, n: (b, 0, n, p))
            prev = pl.BlockSpec((1, dil, BAND_HALF, LANES),
                                lambda b, p, n: (b, 0, jnp.maximum(n * per_blk - 1, 0), p))
            nxt = pl.BlockSpec((1, dil, BAND_HALF, LANES),
                               lambda b, p, n: (b, 0, jnp.minimum((n + 1) * per_blk, last), p))
        return [cur, cur, prev, nxt, cur, prev, nxt]

    args = ([nat(q1)] + [nat(k1)] * 3 + [nat(v1)] * 3 + [q4] + [k4] * 3 + [v4] * 3
            + [q16] + [k16] * 3 + [v16] * 3)
    win_rows = 16 * ATT_KEYS
    out = pl.pallas_call(
        functools.partial(_attn_kernel, nblk),
        grid=(bsz, npair, nblk),
        in_specs=specs(1) + specs(4) + specs(16),
        out_specs=pl.BlockSpec((1, ATT_BLK, LANES), lambda b, p, n: (b, n, p)),
        out_shape=jax.ShapeDtypeStruct((bsz, seq, ATT_WIDTH), BF16),
        scratch_shapes=[pltpu.VMEM((win_rows, LANES), BF16),
                        pltpu.VMEM((win_rows, LANES), BF16),
                        pltpu.VMEM((ATT_BLK, LANES), F32),
                        pltpu.VMEM((ATT_BLK, LANES), F32),
                        pltpu.VMEM((ATT_BLK, LANES), F32),
                        pltpu.VMEM((3, 2 * ATT_Q, ATT_KEYS), F32)],
        compiler_params=_params(("parallel", "parallel", "parallel")),
        name="attn",
    )(*args)
    return out.reshape(bsz * seq, ATT_WIDTH)


def _memkv_kernel(mem_ref, g_ref, wkv_ref, kg_ref, kt_out, v_out):
    mn = _rms(mem_ref[0], g_ref[...]).astype(BF16)
    kv = _dot(mn, wkv_ref[...])
    scale = 1.0 / math.sqrt(XATT_HEAD_DIM)
    for h in range(XATT_HEADS):
        kh = kv[:, h * XATT_HEAD_DIM:(h + 1) * XATT_HEAD_DIM]
        kh = _rms(kh, kg_ref[...]) * scale
        kt_out[0, h * XATT_HEAD_DIM:(h + 1) * XATT_HEAD_DIM, :] = kh.T.astype(BF16)
    v_out[0] = kv[:, D_MODEL:].astype(BF16)


def _memkv(mem, prm):
    bsz = mem.shape[0]
    return pl.pallas_call(
        _memkv_kernel,
        grid=(bsz,),
        in_specs=[pl.BlockSpec((1, MEM_LEN, D_MODEL), lambda b: (b, 0, 0)),
                  _const_spec((1, D_MODEL)), _const_spec((D_MODEL, 2 * D_MODEL)),
                  _const_spec((1, XATT_HEAD_DIM))],
        out_specs=(pl.BlockSpec((1, D_MODEL, MEM_LEN), lambda b: (b, 0, 0)),
                   pl.BlockSpec((1, MEM_LEN, D_MODEL), lambda b: (b, 0, 0))),
        out_shape=(jax.ShapeDtypeStruct((bsz, D_MODEL, MEM_LEN), BF16),
                   jax.ShapeDtypeStruct((bsz, MEM_LEN, D_MODEL), BF16)),
        compiler_params=_params(("parallel",)),
        name="memkv",
    )(mem, prm["mem_g"], prm["w_kv"], prm["xkg"])


def _post_kernel(x_ref, ssd_ref, att_ref, og_ref, wo1_ref, wo2_ref, g_ref, wq_ref, qg_ref, kt_ref,
                 v_ref, wo_ref, h_out):
    att = _rms(att_ref[...].astype(F32), og_ref[...]).astype(BF16)
    h = x_ref[...] + _dot(ssd_ref[...], wo1_ref[...]) + _dot(att, wo2_ref[...])
    hn = _rms(h, g_ref[...]).astype(BF16)
    qf = _dot(hn, wq_ref[...])
    outs = []
    for hd in range(XATT_HEADS):
        sl = slice(hd * XATT_HEAD_DIM, (hd + 1) * XATT_HEAD_DIM)
        qh = _rms(qf[:, sl], qg_ref[...]).astype(BF16)
        sc = _dot(qh, kt_ref[0, sl, :])
        mx = jnp.max(sc, axis=-1, keepdims=True)
        pr = jnp.exp(sc - mx)
        den = jnp.sum(pr, axis=-1, keepdims=True)
        outs.append(_dot(pr.astype(BF16), v_ref[0, :, sl]) / den)
    o = jnp.concatenate(outs, axis=1).astype(BF16)
    h_out[...] = h + _dot(o, wo_ref[...])


def _post(x2, ssd, att, kt, vx, prm, seq, tm):
    t_tokens = x2.shape[0]
    tiles_per_seq = seq // tm
    tok = lambda w: pl.BlockSpec((tm, w), lambda i: (i, 0))
    half = D_MODEL // 2
    return pl.pallas_call(
        _post_kernel,
        grid=(t_tokens // tm,),
        in_specs=[tok(D_MODEL), tok(SSD_WIDTH), tok(ATT_WIDTH), _const_spec((1, ATT_WIDTH)),
                  _const_spec((half, D_MODEL)), _const_spec((half, D_MODEL)),
                  _const_spec((1, D_MODEL)), _const_spec((D_MODEL, D_MODEL)),
                  _const_spec((1, XATT_HEAD_DIM)),
                  pl.BlockSpec((1, D_MODEL, MEM_LEN), lambda i: (i // tiles_per_seq, 0, 0)),
                  pl.BlockSpec((1, MEM_LEN, D_MODEL), lambda i: (i // tiles_per_seq, 0, 0)),
                  _const_spec((D_MODEL, D_MODEL))],
        out_specs=tok(D_MODEL),
        out_shape=jax.ShapeDtypeStruct((t_tokens, D_MODEL), F32),
        compiler_params=_params(("parallel",)),
        name="post",
    )(x2, ssd, att, prm["att_og"], prm["w_out1"], prm["w_out2"], prm["xatt_g"], prm["w_xq"],
      prm["xqg"], kt, vx, prm["w_xo"])


def _mlp_kernel(ff_chunk, h_ref, g_ref, w1_ref, w2_ref, y_out):
    h = h_ref[...]
    hm = _rms(h, g_ref[...]).astype(BF16)
    acc = h
    for c in range(D_FF // ff_chunk):
        sl = slice(c * ff_chunk, (c + 1) * ff_chunk)
        a = jnp.maximum(_dot(hm, w1_ref[:, sl]), 0.0)
        acc = acc + _dot((a * a).astype(BF16), w2_ref[sl, :])
    y_out[...] = acc


def _mlp(h, prm, tm):
    t_tokens = h.shape[0]
    tok = pl.BlockSpec((tm, D_MODEL), lambda i: (i, 0))
    return pl.pallas_call(
        functools.partial(_mlp_kernel, 1024),
        grid=(t_tokens // tm,),
        in_specs=[tok, _const_spec((1, D_MODEL)), _const_spec((D_MODEL, D_FF)),
                  _const_spec((D_FF, D_MODEL))],
        out_specs=tok,
        out_shape=jax.ShapeDtypeStruct((t_tokens, D_MODEL), F32),
        compiler_params=_params(("parallel",)),
        name="mlp",
    )(h, prm["mlp_g"], prm["w1"], prm["w2"])


def _rope_tables(seq):
    half = ROPE_DIMS // 2
    dim = jnp.arange(LANES) % HEAD_DIM
    inv_freq = jnp.power(jnp.float32(ROPE_THETA), -(dim % half).astype(F32) / half)
    ang = jnp.arange(seq).astype(F32)[:, None] * inv_freq[None, :]
    sin = jnp.sin(ang)
    return (jnp.where(dim < ROPE_DIMS, jnp.cos(ang), 1.0),
            jnp.where((dim >= half) & (dim < ROPE_DIMS), sin, 0.0),
            jnp.where(dim < half, -sin, 0.0))


def _prepare(seqs, mix_norm_g, w_in, conv_w, conv_b, ssd_A_log, ssd_dt_bias, ssd_D, ssd_norm_g,
             att_q_norm_g, att_k_norm_g, att_out_norm_g, w_out, xatt_norm_g, mem_norm_g, xatt_wq,
             xatt_wkv, xatt_q_norm_g, xatt_k_norm_g, xatt_wo, mlp_norm_g, mlp_w1, mlp_w2):
    row = lambda t: t.reshape(1, -1).astype(F32)
    pad_lanes = lambda t: jnp.pad(t, ((0, 0), (0, LANES - t.shape[1])))
    head_id = jnp.arange(ATT_WIDTH) // HEAD_DIM
    prm = {
        "mix_g": row(mix_norm_g),
        "w_z": w_in[:, OFF_Z:OFF_XBC].astype(BF16),
        "w_xbc": w_in[:, OFF_XBC:OFF_DT].astype(BF16),
        "w_dt": pad_lanes(w_in[:, OFF_DT:OFF_Q]).astype(BF16),
        "w_q": w_in[:, OFF_Q:OFF_K].astype(BF16),
        "w_k": w_in[:, OFF_K:OFF_V].astype(BF16),
        "w_v": w_in[:, OFF_V:IN_COLS].astype(BF16),
        "head_ones": (head_id[:, None] == head_id[None, :]).astype(BF16),
        "qg": jnp.tile(row(att_q_norm_g), (1, ATT_HEADS)),
        "kg": jnp.tile(row(att_k_norm_g), (1, ATT_HEADS)),
        "conv_w": conv_w.astype(F32),
        "conv_b": row(conv_b),
        "alog_row": pad_lanes(row(ssd_A_log)),
        "dtbias_row": pad_lanes(row(ssd_dt_bias)),
        "d_row": jnp.repeat(row(ssd_D), HEAD_DIM, axis=1),
        "ssd_ng": row(ssd_norm_g),
        "att_og": row(att_out_norm_g),
        "w_out1": w_out[:SSD_WIDTH].astype(BF16),
        "w_out2": w_out[SSD_WIDTH:].astype(BF16),
        "xatt_g": row(xatt_norm_g),
        "mem_g": row(mem_norm_g),
        "w_xq": xatt_wq.astype(BF16),
        "w_kv": xatt_wkv.astype(BF16),
        "xqg": row(xatt_q_norm_g),
        "xkg": row(xatt_k_norm_g),
        "w_xo": xatt_wo.astype(BF16),
        "mlp_g": row(mlp_norm_g),
        "w1": mlp_w1.astype(BF16),
        "w2": mlp_w2.astype(BF16),
        "rope_cos": {}, "rope_sdn": {}, "rope_sup": {},
    }
    for seq in set(seqs):
        cos, sdn, sup = _rope_tables(seq)
        prm["rope_cos"][seq], prm["rope_sdn"][seq], prm["rope_sup"][seq] = cos, sdn, sup
    return prm


def _token_tile(seq):
    return min(512, seq)


def _layer(x, mem, prm):
    bsz, seq, _ = x.shape
    tm = _token_tile(seq)
    x2 = x.reshape(bsz * seq, D_MODEL)
    outs = _inproj(x2, bsz, seq, prm, tm)
    z, xbc, dt = outs[:3]
    ssd = _ssd(z, xbc, dt, prm, bsz, seq)
    att = _dilated_attention((outs[3:6], outs[6:9], outs[9:12]), bsz, seq)
    kt, vx = _memkv(mem, prm)
    h = _post(x2, ssd, att, kt, vx, prm, seq, tm)
    y = _mlp(h, prm, tm)
    return y.reshape(bsz, seq, D_MODEL)


def kernel(x_prompt, x_sample, mem_prompt, mem_sample, mix_norm_g, w_in, conv_w, conv_b, ssd_A_log,
           ssd_dt_bias, ssd_D, ssd_norm_g, att_q_norm_g, att_k_norm_g, att_out_norm_g, w_out,
           xatt_norm_g, mem_norm_g, xatt_wq, xatt_wkv, xatt_q_norm_g, xatt_k_norm_g, xatt_wo,
           mlp_norm_g, mlp_w1, mlp_w2):
    weights = (mix_norm_g, w_in, conv_w, conv_b, ssd_A_log, ssd_dt_bias, ssd_D, ssd_norm_g,
               att_q_norm_g, att_k_norm_g, att_out_norm_g, w_out, xatt_norm_g, mem_norm_g, xatt_wq,
               xatt_wkv, xatt_q_norm_g, xatt_k_norm_g, xatt_wo, mlp_norm_g, mlp_w1, mlp_w2)
    assert all(w.shape[0] == 1 for w in weights), "single-layer stack expected"
    prm = _prepare((x_prompt.shape[1], x_sample.shape[1]), *(w[0] for w in weights))
    return (_layer(x_prompt, mem_prompt, prm), _layer(x_sample, mem_sample, prm))
```

```python
import functools
import math

import jax
import jax.numpy as jnp
from jax import lax
from jax.experimental import pallas as pl
from jax.experimental.pallas import tpu as pltpu

F32 = jnp.float32
BF16 = jnp.bfloat16

D_MODEL = 1024
HEAD_DIM = 64
SSD_HEADS = 8
SSD_WIDTH = SSD_HEADS * HEAD_DIM
SSD_GROUPS = 2
SSD_STATE = 128
SSD_CONV = 5
SSD_CHUNK = 128
ATT_HEADS = 8
ATT_WIDTH = ATT_HEADS * HEAD_DIM
DILATED_PATTERNS = ((128, 1), (512, 4), (2048, 16))
ROPE_DIMS = HEAD_DIM // 4
ROPE_THETA = 500000.0
MEM_LEN = 256
XATT_HEADS = 4
XATT_HEAD_DIM = D_MODEL // XATT_HEADS
D_FF = 4 * D_MODEL
EPS = 1e-6
NEG_INF = -1e30
CONV_CH = SSD_WIDTH + 2 * SSD_GROUPS * SSD_STATE
OFF_Z = 0
OFF_XBC = OFF_Z + SSD_WIDTH
OFF_DT = OFF_XBC + CONV_CH
OFF_Q = OFF_DT + 2 * SSD_HEADS
OFF_K = OFF_Q + ATT_WIDTH
OFF_V = OFF_K + ATT_WIDTH
IN_COLS = OFF_V + ATT_WIDTH

LANES = 128
BF16_ROWS = 16
VMEM_LIMIT = 56 * 1024 * 1024
SSD_CPS = 8
BAND_HALF = 64
ATT_Q = 128
ATT_KEYS = ATT_Q + 2 * BAND_HALF
ATT_BLK = 2048
ATT_GROUP = 16
LOG2E = math.log2(math.e)


def _dot(a, b):
    return jnp.dot(a, b, preferred_element_type=F32)


def _dot_nt(a, b):
    return lax.dot_general(a, b, (((1,), (1,)), ((), ())), preferred_element_type=F32)


def _const_spec(shape):
    zeros = (0,) * len(shape)
    return pl.BlockSpec(shape, lambda *_: zeros, pipeline_mode=pl.Buffered(1))


def _params(sem):
    return pltpu.CompilerParams(dimension_semantics=sem, vmem_limit_bytes=VMEM_LIMIT)


def _lane(shape):
    return lax.broadcasted_iota(jnp.int32, shape, len(shape) - 1)


def _rms(x, g):
    ms = jnp.mean(x * x, axis=-1, keepdims=True)
    return x * lax.rsqrt(ms + EPS) * g


def _silu(x):
    h = 0.5 * x
    return h + h * jnp.tanh(h)


def _emit_layouts(t, nat_out, c4_out, c16_out, slab_ref, slab2_ref):
    tm = t.shape[0]
    n4, n16 = tm // 4, tm // 16
    nat_out[...] = t.astype(BF16)
    for p in range(ATT_WIDTH // LANES):
        sl = slice(p * LANES, (p + 1) * LANES)
        slab_ref[p] = t[:, sl]
        for r4 in range(4):
            c4 = slab_ref[p, pl.ds(r4, n4, stride=4), :]
            c4_out[0, r4, :, sl] = c4.astype(BF16)
            slab2_ref[p, r4 * n4:(r4 + 1) * n4, :] = c4
        for r4 in range(4):
            for a in range(4):
                c16 = slab2_ref[p, pl.ds(r4 * n4 + a, n16, stride=4), :]
                c16_out[0, 4 * a + r4, :, sl] = c16.astype(BF16)


def _inproj_kernel(x_ref, g_ref, wz_ref, wxbc_ref, wdt_ref, wq_ref, wk_ref, wv_ref, ones_ref,
                   qg_ref, kg_ref, rope_ref,
                   xz_out, dt_out, q1_out, q4_out, q16_out, kv1_out, kv4_out, kv16_out,
                   slab_ref, slab2_ref):
    xn = _rms(x_ref[...], g_ref[...]).astype(BF16)

    cos = jnp.concatenate([rope_ref[:, 0:LANES]] * 4, axis=1)
    sdn = jnp.concatenate([rope_ref[:, LANES:2 * LANES]] * 4, axis=1)
    sup = jnp.concatenate([rope_ref[:, 2 * LANES:]] * 4, axis=1)

    def norm_rope(w_ref, gain_ref, scale):
        t = _dot(xn, w_ref[...])
        ssq = _dot((t * t).astype(BF16), ones_ref[...])
        t = t * lax.rsqrt(ssq * (1.0 / HEAD_DIM) + EPS) * gain_ref[...]
        half = ROPE_DIMS // 2
        t = (t * cos + pltpu.roll(t, half, 1) * sdn
             + pltpu.roll(t, ATT_WIDTH - half, 1) * sup)
        return t * scale

    _emit_layouts(norm_rope(wq_ref, qg_ref, LOG2E / math.sqrt(HEAD_DIM)), q1_out, q4_out, q16_out,
                  slab_ref.at[0], slab2_ref.at[0])
    xz_out[:, CONV_CH:] = _dot(xn, wz_ref[...]).astype(BF16)
    _emit_layouts(norm_rope(wk_ref, kg_ref, 1.0), kv1_out.at[0], kv4_out.at[0], kv16_out.at[0],
                  slab_ref.at[1], slab2_ref.at[1])
    xz_out[:, :CONV_CH] = _dot(xn, wxbc_ref[...]).astype(BF16)
    _emit_layouts(_dot(xn, wv_ref[...]), kv1_out.at[1], kv4_out.at[1], kv16_out.at[1],
                  slab_ref.at[2], slab2_ref.at[2])
    dt_out[...] = _dot(xn, wdt_ref[...])


def _inproj(x2, bsz, seq, prm, tm):
    t_tokens = x2.shape[0]
    tiles_per_seq = seq // tm
    tok = lambda w: pl.BlockSpec((tm, w), lambda i: (i, 0))
    rope = pl.BlockSpec((tm, 3 * LANES), lambda i: (i % tiles_per_seq, 0))
    cls = lambda d: pl.BlockSpec((1, d, tm // d, ATT_WIDTH),
                                 lambda i: (i // tiles_per_seq, 0, i % tiles_per_seq, 0))
    cls2 = lambda d: pl.BlockSpec((2, 1, d, tm // d, ATT_WIDTH),
                                  lambda i: (0, i // tiles_per_seq, 0, i % tiles_per_seq, 0))
    cls_shape = lambda d: (bsz, d, seq // d, ATT_WIDTH)
    out_shape = (
        jax.ShapeDtypeStruct((t_tokens, CONV_CH + SSD_WIDTH), BF16),
        jax.ShapeDtypeStruct((t_tokens, LANES), F32),
        jax.ShapeDtypeStruct((t_tokens, ATT_WIDTH), BF16),
        jax.ShapeDtypeStruct(cls_shape(4), BF16),
        jax.ShapeDtypeStruct(cls_shape(16), BF16),
        jax.ShapeDtypeStruct((2, t_tokens, ATT_WIDTH), BF16),
        jax.ShapeDtypeStruct((2,) + cls_shape(4), BF16),
        jax.ShapeDtypeStruct((2,) + cls_shape(16), BF16),
    )
    out_specs = (tok(CONV_CH + SSD_WIDTH), tok(LANES), tok(ATT_WIDTH), cls(4), cls(16),
                 pl.BlockSpec((2, tm, ATT_WIDTH), lambda i: (0, i, 0)), cls2(4), cls2(16))
    return pl.pallas_call(
        _inproj_kernel,
        grid=(t_tokens // tm,),
        in_specs=[
            tok(D_MODEL), _const_spec((1, D_MODEL)),
            _const_spec((D_MODEL, SSD_WIDTH)), _const_spec((D_MODEL, CONV_CH)),
            _const_spec((D_MODEL, LANES)), _const_spec((D_MODEL, ATT_WIDTH)),
            _const_spec((D_MODEL, ATT_WIDTH)), _const_spec((D_MODEL, ATT_WIDTH)),
            _const_spec((ATT_WIDTH, ATT_WIDTH)),
            _const_spec((1, ATT_WIDTH)), _const_spec((1, ATT_WIDTH)),
            rope,
        ],
        out_specs=out_specs,
        out_shape=out_shape,
        scratch_shapes=[pltpu.VMEM((3, ATT_WIDTH // LANES, tm, LANES), F32),
                        pltpu.VMEM((3, ATT_WIDTH // LANES, tm, LANES), F32)],
        compiler_params=_params(("parallel",)),
        name="inproj",
    )(x2, prm["mix_g"], prm["w_z"], prm["w_xbc"], prm["w_dt"], prm["w_q"], prm["w_k"], prm["w_v"],
      prm["head_ones"], prm["qg"], prm["kg"], prm["rope"][seq])


def _split3(a):
    a1 = a.astype(BF16)
    r1 = a - a1.astype(F32)
    a2 = r1.astype(BF16)
    a3 = (r1 - a2.astype(F32)).astype(BF16)
    return a1, a2, a3


def _dot_split(lhs_bf16, a):
    a1, a2, a3 = _split3(a)
    return _dot(lhs_bf16, a1) + _dot(lhs_bf16, a2) + _dot(lhs_bf16, a3)


def _ssd_kernel(nsteps, xbc_ref, xprev_ref, xnext_ref, z_ref, dt_ref, convw_ref, convb_ref, alog_ref,
                dtbias_ref, drow_ref, ng_ref, out_ref, ext_ref, carry_ref, prevb_ref, conv_ref):
    q = SSD_CHUNK
    nh = SSD_HEADS
    width = SSD_WIDTH // SSD_GROUPS
    pad = (SSD_CONV - 1) // 2
    s = pl.program_id(1)
    backward = s < nsteps
    step = jnp.where(backward, nsteps - 1 - s, s - nsteps)

    lane = _lane((q, LANES))
    row = lax.broadcasted_iota(jnp.int32, (q, LANES), 0)
    erow = lax.broadcasted_iota(jnp.int32, (LANES, 2 * SSD_WIDTH), 0)
    ecol = lax.broadcasted_iota(jnp.int32, (LANES, 2 * SSD_WIDTH), 1)
    expand = (erow == ecol // HEAD_DIM).astype(BF16)
    ltri = (lane <= row).astype(BF16)
    utri = (lane >= row).astype(BF16)

    def expand_rows(v, lo, hi):
        return _dot(v.astype(BF16), expand[:, lo:hi])

    def expand_row_exact(v, lo, hi):
        v1, v2, v3 = _split3(jnp.broadcast_to(v, (8, LANES)))
        e = expand[:, lo:hi]
        return (_dot(v1, e) + _dot(v2, e) + _dot(v3, e))[0:1]

    def step_decay(j):
        raw = dt_ref[0, j * q:(j + 1) * q, :] + dtbias_ref[...]
        softplus = jnp.maximum(raw, 0.0) + jnp.log(1.0 + jnp.exp(-jnp.abs(raw)))
        dt = jnp.where(lane < 2 * nh, softplus, 0.0)
        return dt, dt * (-jnp.exp(alog_ref[...]))

    def chunk_states(xs, bm16, w_full):
        xw = (xs * w_full).astype(BF16)
        parts = []
        for g in range(SSD_GROUPS):
            bt = bm16[:, g * SSD_STATE:(g + 1) * SSD_STATE].astype(F32).T.astype(BF16)
            parts.append(_dot(bt, xw[:, g * width:(g + 1) * width]))
        return jnp.concatenate(parts, axis=1)

    @pl.when(s == 0)
    def _():
        carry_ref[...] = jnp.zeros_like(carry_ref)

    @pl.when(s == nsteps)
    def _():
        carry_ref[...] = jnp.zeros_like(carry_ref)

    @pl.when(backward)
    def _():
        rows = SSD_CPS * q
        prev = xprev_ref[0]
        nxt = xnext_ref[0]
        ext_ref[0:BF16_ROWS, :] = jnp.where(step > 0, prev, jnp.zeros_like(prev))
        ext_ref[BF16_ROWS:BF16_ROWS + rows, :] = xbc_ref[0]
        ext_ref[BF16_ROWS + rows:, :] = jnp.where(step < nsteps - 1, nxt, jnp.zeros_like(nxt))

        win_rows = q + 2 * BF16_ROWS
        srow = lax.broadcasted_iota(jnp.int32, (q, win_rows), 0)
        scol = lax.broadcasted_iota(jnp.int32, (q, win_rows), 1)
        shifts = {k: (scol == srow + BF16_ROWS + k - pad).astype(BF16)
                  for k in range(SSD_CONV) if k != pad}

        bmask = (lane >= nh) & (lane < 2 * nh)

        def decay_terms(j):
            dt, a = step_decay(j)
            sb = _dot_split(utri, a)
            wb = jnp.where(bmask, jnp.exp(sb[0:1, :] - sb) * dt, 0.0)
            decay = expand_row_exact(jnp.where(bmask[0:1], jnp.exp(sb[0:1, :]), 0.0),
                                     SSD_WIDTH, 2 * SSD_WIDTH)
            return expand_rows(wb, SSD_WIDTH, 2 * SSD_WIDTH), decay

        def conv_dots(j):
            win = ext_ref[j * q:j * q + win_rows, :]
            return win, {k: _dot(sh, win) for k, sh in shifts.items()}

        def conv_finish(j, win, dots):
            acc = convb_ref[...] + win[BF16_ROWS:BF16_ROWS + q].astype(F32) * convw_ref[pad:pad + 1, :]
            for k, d in dots.items():
                acc = acc + d * convw_ref[k:k + 1, :]
            conv_ref[step * SSD_CPS + j] = _silu(acc).astype(BF16)

        pending = conv_dots(0)
        terms = []
        for j in range(SSD_CPS):
            ahead = conv_dots(j + 1) if j + 1 < SSD_CPS else None
            terms.append(decay_terms(j))
            conv_finish(j, *pending)
            pending = ahead

        states = []
        for j in range(SSD_CPS):
            xb16 = conv_ref[step * SSD_CPS + j]
            states.append(chunk_states(xb16[:, :SSD_WIDTH].astype(F32),
                                       xb16[:, SSD_WIDTH:SSD_WIDTH + SSD_GROUPS * SSD_STATE],
                                       terms[j][0]))
        carry = carry_ref[...]
        for j in reversed(range(SSD_CPS)):
            prevb_ref[step * SSD_CPS + j] = carry.astype(BF16)
            carry = carry * terms[j][1] + states[j]
        carry_ref[...] = carry

    @pl.when(jnp.logical_not(backward))
    def _():
        ii = lax.broadcasted_iota(jnp.int32, (q, q), 0)
        jj = lax.broadcasted_iota(jnp.int32, (q, q), 1)
        lane_p = _lane((q, LANES))
        fmask = lane < nh
        pre = []
        prevs = []
        carry = carry_ref[...]

        def decay_terms(j):
            dt, a = step_decay(j)
            cf = _dot_split(ltri, a)
            sb = _dot_split(utri, a)
            ldt = jnp.where(lane < 2 * nh, jnp.log2(dt), 0.0)
            pack = jnp.where(fmask, cf * LOG2E,
                             jnp.where(lane < 2 * nh, sb * LOG2E, pltpu.roll(ldt, 2 * nh, 1)))
            pack_t = pack.T
            dsum = dt + pltpu.roll(dt, LANES - nh, 1)
            dsum_t = jnp.where(fmask, jnp.log2(dsum), 0.0).T
            edge = expand_rows(jnp.where(lane < 2 * nh, jnp.exp2(pack), 0.0), 0, 2 * SSD_WIDTH)
            wf = jnp.where(fmask, jnp.exp(cf[q - 1:q, :] - cf) * dt, 0.0)
            decay = expand_row_exact(jnp.where(fmask[0:1], jnp.exp(cf[q - 1:q, :]), 0.0),
                                     0, SSD_WIDTH)
            pre.append((pack, pack_t, dsum_t, edge, expand_rows(wf, 0, SSD_WIDTH), decay))

        for j in range(SSD_CPS):
            decay_terms(j)
        for j in range(SSD_CPS):
            xb16 = conv_ref[step * SSD_CPS + j]
            st = chunk_states(xb16[:, :SSD_WIDTH].astype(F32),
                              xb16[:, SSD_WIDTH:SSD_WIDTH + SSD_GROUPS * SSD_STATE], pre[j][4])
            prevs.append(carry.astype(BF16))
            carry = carry * pre[j][5] + st
        carry_ref[...] = carry

        for j in range(SSD_CPS):
            c = step * SSD_CPS + j
            pack, pack_t, dsum_t, edge = pre[j][:4]
            xb16 = conv_ref[c]
            xs16 = xb16[:, :SSD_WIDTH]
            xs = xs16.astype(F32)
            bm16 = xb16[:, SSD_WIDTH:SSD_WIDTH + SSD_GROUPS * SSD_STATE]
            cm16 = xb16[:, SSD_WIDTH + SSD_GROUPS * SSD_STATE:]

            cb = [_dot_nt(cm16[:, g * SSD_STATE:(g + 1) * SSD_STATE],
                          bm16[:, g * SSD_STATE:(g + 1) * SSD_STATE]) for g in range(SSD_GROUPS)]
            mats = []
            for h in range(nh):
                low_row = pack_t[h:h + 1, :] - pack_t[2 * nh + h:2 * nh + h + 1, :]
                up_row = pack_t[nh + h:nh + h + 1, :] - pack_t[3 * nh + h:3 * nh + h + 1, :]
                expo = jnp.where(jj < ii, pack[:, h:h + 1] - low_row,
                                 jnp.where(jj > ii, pack[:, nh + h:nh + h + 1] - up_row,
                                           dsum_t[h:h + 1, :]))
                mats.append((cb[h // (nh // SSD_GROUPS)] * jnp.exp2(expo)).astype(BF16))

            y_parts = []
            for p in range(nh // 2):
                xp = xs16[:, p * LANES:(p + 1) * LANES]
                zero = jnp.zeros_like(xp)
                rhs = jnp.concatenate([jnp.where(lane_p < HEAD_DIM, xp, zero),
                                       jnp.where(lane_p >= HEAD_DIM, xp, zero)], axis=0)
                lhs = jnp.concatenate([mats[2 * p], mats[2 * p + 1]], axis=1)
                y_parts.append(_dot(lhs, rhs))
            y = jnp.concatenate(y_parts, axis=1)

            prev_f = prevs[j]
            prev_b = prevb_ref[c]
            off_parts = []
            for g in range(SSD_GROUPS):
                st = jnp.concatenate([prev_f[:, g * width:(g + 1) * width],
                                      prev_b[:, g * width:(g + 1) * width]], axis=1)
                res = _dot(cm16[:, g * SSD_STATE:(g + 1) * SSD_STATE], st)
                off_parts.append(res[:, :width] * edge[:, g * width:(g + 1) * width]
                                 + res[:, width:] * edge[:, SSD_WIDTH + g * width:
                                                         SSD_WIDTH + (g + 1) * width])
            y = y + jnp.concatenate(off_parts, axis=1) + drow_ref[...] * xs

            y = y * _silu(z_ref[0, j * q:(j + 1) * q, :].astype(F32))
            out_ref[0, j * q:(j + 1) * q, :] = _rms(y, ng_ref[...]).astype(BF16)


def _ssd(xz, dt, prm, bsz, seq):
    q = SSD_CHUNK
    rows = SSD_CPS * q
    assert seq % rows == 0
    nsteps = seq // rows
    nc = seq // q
    halo_blocks = rows // BF16_ROWS
    conv_step = lambda s: jnp.where(s < nsteps, nsteps - 1 - s, 0)
    both_step = lambda s: jnp.where(s < nsteps, nsteps - 1 - s, s - nsteps)
    fwd_step = lambda s: jnp.maximum(s - nsteps, 0)
    xz3 = xz.reshape(bsz, seq, CONV_CH + SSD_WIDTH)
    gate_blk = CONV_CH // SSD_WIDTH
    dt3 = dt.reshape(bsz, seq, LANES)
    in_specs = [
        pl.BlockSpec((1, rows, CONV_CH), lambda b, s: (b, conv_step(s), 0)),
        pl.BlockSpec((1, BF16_ROWS, CONV_CH),
                     lambda b, s: (b, jnp.maximum(conv_step(s) * halo_blocks - 1, 0), 0)),
        pl.BlockSpec((1, BF16_ROWS, CONV_CH),
                     lambda b, s: (b, jnp.minimum((conv_step(s) + 1) * halo_blocks,
                                                  seq // BF16_ROWS - 1), 0)),
        pl.BlockSpec((1, rows, SSD_WIDTH), lambda b, s: (b, fwd_step(s), gate_blk)),
        pl.BlockSpec((1, rows, LANES), lambda b, s: (b, both_step(s), 0)),
        _const_spec((SSD_CONV, CONV_CH)), _const_spec((1, CONV_CH)),
        _const_spec((1, LANES)), _const_spec((1, LANES)),
        _const_spec((1, SSD_WIDTH)), _const_spec((1, SSD_WIDTH)),
    ]
    out_spec = pl.BlockSpec((1, rows, SSD_WIDTH), lambda b, s: (b, fwd_step(s), 0))
    out = pl.pallas_call(
        functools.partial(_ssd_kernel, nsteps),
        grid=(bsz, 2 * nsteps),
        in_specs=in_specs,
        out_specs=out_spec,
        out_shape=jax.ShapeDtypeStruct((bsz, seq, SSD_WIDTH), BF16),
        scratch_shapes=[
            pltpu.VMEM((rows + 2 * BF16_ROWS, CONV_CH), BF16),
            pltpu.VMEM((SSD_STATE, SSD_WIDTH), F32),
            pltpu.VMEM((nc, SSD_STATE, SSD_WIDTH), BF16),
            pltpu.VMEM((nc, q, CONV_CH), BF16),
        ],
        compiler_params=_params(("parallel", "arbitrary")),
        name="ssd",
    )(xz3, xz3, xz3, xz3, dt3, prm["conv_w"], prm["conv_b"], prm["alog_row"], prm["dtbias_row"],
      prm["d_row"], prm["ssd_ng"])
    return out.reshape(bsz * seq, SSD_WIDTH)


def _attn_unit(q_tile, kwin_ref, vwin_ref, row0, bias, m_old):
    low = _lane((ATT_Q, LANES)) < HEAD_DIM
    zero = jnp.zeros_like(q_tile)
    lhs = jnp.concatenate([jnp.where(low, q_tile, zero), jnp.where(low, zero, q_tile)], axis=0)
    sc = _dot_nt(lhs, kwin_ref[pl.ds(row0, ATT_KEYS), :]) + bias
    part = jnp.maximum(sc[:, :LANES], sc[:, LANES:])
    if m_old is not None:
        ninf = jnp.full_like(m_old, NEG_INF)
        part = jnp.maximum(part, jnp.concatenate([jnp.where(low, m_old, ninf),
                                                  jnp.where(low, ninf, m_old)], axis=0))
    mx = jnp.max(part, axis=-1, keepdims=True)
    pr = jnp.exp2(sc - mx).astype(BF16)
    ones = jnp.ones((ATT_KEYS, LANES), BF16)
    vcat = jnp.concatenate([vwin_ref[pl.ds(row0, ATT_KEYS), :], ones], axis=1)
    res = _dot(pr, vcat)
    num = jnp.where(low, res[:ATT_Q, :LANES], res[ATT_Q:, :LANES])
    den = jnp.where(low, res[:ATT_Q, LANES:], res[ATT_Q:, LANES:])
    mxp = jnp.where(low, jnp.broadcast_to(mx[:ATT_Q], (ATT_Q, LANES)),
                    jnp.broadcast_to(mx[ATT_Q:], (ATT_Q, LANES)))
    return num, den, mxp


def _attn_kernel(nblk, q1, kv1c, kv1p, kv1n, q4, kv4c, kv4p, kv4n, q16, kv16c, kv16p, kv16n,
                 o_out, kwin, vwin, acc_ref, den_ref, max_ref, bias_ref):
    k1c, k1p, k1n, v1c, v1p, v1n = (r.at[i] for i in range(2) for r in (kv1c, kv1p, kv1n))
    k4c, k4p, k4n, v4c, v4p, v4n = (r.at[i] for i in range(2) for r in (kv4c, kv4p, kv4n))
    k16c, k16p, k16n, v16c, v16p, v16n = (r.at[i] for i in range(2) for r in (kv16c, kv16p, kv16n))
    n = pl.program_id(2)
    half = BAND_HALF
    rows2 = 2 * ATT_Q
    qi = lax.broadcasted_iota(jnp.int32, (rows2, ATT_KEYS), 0) % ATT_Q
    kt = lax.broadcasted_iota(jnp.int32, (rows2, ATT_KEYS), 1)
    band = jnp.where(jnp.abs(kt - half - qi) <= half, 0.0, NEG_INF)
    bias_ref[0] = band
    bias_ref[1] = jnp.where(kt < half, NEG_INF, band)
    bias_ref[2] = jnp.where(kt >= half + ATT_Q, NEG_INF, band)

    def bias_for(first_sub, last_sub):
        idx = jnp.where((n == 0) & first_sub, 1, jnp.where((n == nblk - 1) & last_sub, 2, 0))
        return bias_ref[idx]

    def fill(win, base, prev, cur, nxt, rows):
        win[pl.ds(base, half), :] = prev
        win[pl.ds(base + half, rows), :] = cur
        win[pl.ds(base + half + rows, half), :] = nxt

    def load_state(rows):
        return acc_ref[rows, :], den_ref[rows, :], max_ref[rows, :]

    def merge(old, num, den, mxp):
        scale = jnp.exp2(old[2] - mxp)
        return old[0] * scale + num, old[1] * scale + den, mxp

    def store_state(rows, new):
        acc_ref[rows, :] = new[0]
        den_ref[rows, :] = new[1]
        max_ref[rows, :] = new[2]

    span16 = ATT_KEYS

    def fill16(r, carry):
        base = pl.multiple_of(r * span16, span16)
        fill(kwin, base, k16p[0, r], k16c[0, r], k16n[0, r], ATT_Q)
        fill(vwin, base, v16p[0, r], v16c[0, r], v16n[0, r], ATT_Q)
        return carry

    lax.fori_loop(0, 16, fill16, 0)

    def group16(g, carry):
        new = []
        for u in range(ATT_GROUP):
            r = g * ATT_GROUP + u
            new.append(_attn_unit(q16[0, r], kwin, vwin, pl.multiple_of(r * span16, span16),
                                  bias_for(True, True), None))
        for u in range(ATT_GROUP):
            store_state(pl.ds(g * ATT_GROUP + u, ATT_Q, stride=16), new[u])
        return carry

    lax.fori_loop(0, 16 // ATT_GROUP, group16, 0)

    rows4 = ATT_BLK // 4
    nsub4 = rows4 // ATT_Q
    span4 = rows4 + 2 * half
    for r in range(4):
        fill(kwin, r * span4, k4p[0, r], k4c[0, r], k4n[0, r], rows4)
        fill(vwin, r * span4, v4p[0, r], v4c[0, r], v4n[0, r], rows4)

    ncls = ATT_GROUP // nsub4

    def group4(g, carry):
        units = [(g * ncls + u, j) for u in range(ncls) for j in range(nsub4)]
        rows = [pl.ds(r + 4 * ATT_Q * j, ATT_Q, stride=4) for r, j in units]
        old = [load_state(rw) for rw in rows]
        new = []
        for (r, j), prev in zip(units, old):
            unit = _attn_unit(q4[0, r, j * ATT_Q:(j + 1) * ATT_Q, :], kwin, vwin,
                              pl.multiple_of(r * span4, ATT_Q) + j * ATT_Q,
                              bias_for(j == 0, j == nsub4 - 1), prev[2])
            new.append(merge(prev, *unit))
        for rw, nw in zip(rows, new):
            store_state(rw, nw)
        return carry

    lax.fori_loop(0, 4 // ncls, group4, 0)

    fill(kwin, 0, k1p[0], k1c[0], k1n[0], ATT_BLK)
    fill(vwin, 0, v1p[0], v1c[0], v1n[0], ATT_BLK)
    nsub1 = ATT_BLK // ATT_Q

    def group1(g, carry):
        for u in range(ATT_GROUP):
            j = g * ATT_GROUP + u
            r0 = pl.multiple_of(j * ATT_Q, ATT_Q)
            rows = pl.ds(r0, ATT_Q)
            old = load_state(rows)
            unit = _attn_unit(q1[0, rows, :], kwin, vwin, r0, bias_for(j == 0, j == nsub1 - 1),
                              old[2])
            acc_new, den_new, _ = merge(old, *unit)
            o_out[0, rows, :] = (acc_new / den_new).astype(BF16)
        return carry

    lax.fori_loop(0, nsub1 // ATT_GROUP, group1, 0)


def _dilated_attention(qkv, bsz, seq):
    (q1, q4, q16), (kv1, kv4, kv16) = qkv
    assert tuple(d for _, d in DILATED_PATTERNS) == (1, 4, 16)
    assert all(w // (2 * d) == BAND_HALF for w, d in DILATED_PATTERNS)
    assert seq % ATT_BLK == 0 and seq // 16 >= 2 * ATT_Q
    nblk = seq // ATT_BLK
    npair = ATT_WIDTH // LANES
    def specs(dil):
        rows = ATT_BLK // dil
        per_blk = rows // BAND_HALF
        last = seq // dil // BAND_HALF - 1
        cls = () if dil == 1 else (dil,)
        zero = () if dil == 1 else (0,)
        before = lambda n: jnp.maximum(n * per_blk - 1, 0)
        after = lambda n: jnp.minimum((n + 1) * per_blk, last)
        return [
            pl.BlockSpec((1,) + cls + (rows, LANES), lambda b, p, n: (b,) + zero + (n, p)),
            pl.BlockSpec((2, 1) + cls + (rows, LANES), lambda b, p, n: (0, b) + zero + (n, p)),
            pl.BlockSpec((2, 1) + cls + (BAND_HALF, LANES),
                         lambda b, p, n: (0, b) + zero + (before(n), p)),
            pl.BlockSpec((2, 1) + cls + (BAND_HALF, LANES),
                         lambda b, p, n: (0, b) + zero + (after(n), p)),
        ]

    kv1 = kv1.reshape(2, bsz, seq, ATT_WIDTH)
    args = ([q1.reshape(bsz, seq, ATT_WIDTH)] + [kv1] * 3 + [q4] + [kv4] * 3 + [q16] + [kv16] * 3)
    win_rows = 16 * ATT_KEYS
    out = pl.pallas_call(
        functools.partial(_attn_kernel, nblk),
        grid=(bsz, npair, nblk),
        in_specs=specs(1) + specs(4) + specs(16),
        out_specs=pl.BlockSpec((1, ATT_BLK, LANES), lambda b, p, n: (b, n, p)),
        out_shape=jax.ShapeDtypeStruct((bsz, seq, ATT_WIDTH), BF16),
        scratch_shapes=[pltpu.VMEM((win_rows, LANES), BF16),
                        pltpu.VMEM((win_rows, LANES), BF16),
                        pltpu.VMEM((ATT_BLK, LANES), F32),
                        pltpu.VMEM((ATT_BLK, LANES), F32),
                        pltpu.VMEM((ATT_BLK, LANES), F32),
                        pltpu.VMEM((3, 2 * ATT_Q, ATT_KEYS), F32)],
        compiler_params=_params(("parallel", "parallel", "parallel")),
        name="attn",
    )(*args)
    return out.reshape(bsz * seq, ATT_WIDTH)


def _memkv_kernel(mem_ref, g_ref, wkv_ref, kg_ref, kt_out, v_out):
    mn = _rms(mem_ref[0], g_ref[...]).astype(BF16)
    kv = _dot(mn, wkv_ref[...])
    scale = 1.0 / math.sqrt(XATT_HEAD_DIM)
    for h in range(XATT_HEADS):
        kh = kv[:, h * XATT_HEAD_DIM:(h + 1) * XATT_HEAD_DIM]
        kh = _rms(kh, kg_ref[...]) * scale
        kt_out[0, h * XATT_HEAD_DIM:(h + 1) * XATT_HEAD_DIM, :] = kh.T.astype(BF16)
    v_out[0] = kv[:, D_MODEL:].astype(BF16)


def _memkv(mem, prm):
    bsz = mem.shape[0]
    return pl.pallas_call(
        _memkv_kernel,
        grid=(bsz,),
        in_specs=[pl.BlockSpec((1, MEM_LEN, D_MODEL), lambda b: (b, 0, 0)),
                  _const_spec((1, D_MODEL)), _const_spec((D_MODEL, 2 * D_MODEL)),
                  _const_spec((1, XATT_HEAD_DIM))],
        out_specs=(pl.BlockSpec((1, D_MODEL, MEM_LEN), lambda b: (b, 0, 0)),
                   pl.BlockSpec((1, MEM_LEN, D_MODEL), lambda b: (b, 0, 0))),
        out_shape=(jax.ShapeDtypeStruct((bsz, D_MODEL, MEM_LEN), BF16),
                   jax.ShapeDtypeStruct((bsz, MEM_LEN, D_MODEL), BF16)),
        compiler_params=_params(("parallel",)),
        name="memkv",
    )(mem, prm["mem_g"], prm["w_kv"], prm["xkg"])


def _post_kernel(x_ref, ssd_ref, att_ref, og_ref, wo1_ref, wo2_ref, g_ref, wq_ref, qg_ref, kt_ref,
                 v_ref, wo_ref, h_out):
    att = _rms(att_ref[...].astype(F32), og_ref[...]).astype(BF16)
    h = x_ref[...] + _dot(ssd_ref[...], wo1_ref[...]) + _dot(att, wo2_ref[...])
    hn = _rms(h, g_ref[...]).astype(BF16)
    qf = _dot(hn, wq_ref[...])
    outs = []
    for hd in range(XATT_HEADS):
        sl = slice(hd * XATT_HEAD_DIM, (hd + 1) * XATT_HEAD_DIM)
        qh = _rms(qf[:, sl], qg_ref[...]).astype(BF16)
        sc = _dot(qh, kt_ref[0, sl, :])
        mx = jnp.max(sc, axis=-1, keepdims=True)
        pr = jnp.exp(sc - mx)
        den = jnp.sum(pr, axis=-1, keepdims=True)
        outs.append(_dot(pr.astype(BF16), v_ref[0, :, sl]) / den)
    o = jnp.concatenate(outs, axis=1).astype(BF16)
    h_out[...] = h + _dot(o, wo_ref[...])


def _post(x2, ssd, att, kt, vx, prm, seq, tm):
    t_tokens = x2.shape[0]
    tiles_per_seq = seq // tm
    tok = lambda w: pl.BlockSpec((tm, w), lambda i: (i, 0))
    half = D_MODEL // 2
    return pl.pallas_call(
        _post_kernel,
        grid=(t_tokens // tm,),
        in_specs=[tok(D_MODEL), tok(SSD_WIDTH), tok(ATT_WIDTH), _const_spec((1, ATT_WIDTH)),
                  _const_spec((half, D_MODEL)), _const_spec((half, D_MODEL)),
                  _const_spec((1, D_MODEL)), _const_spec((D_MODEL, D_MODEL)),
                  _const_spec((1, XATT_HEAD_DIM)),
                  pl.BlockSpec((1, D_MODEL, MEM_LEN), lambda i: (i // tiles_per_seq, 0, 0)),
                  pl.BlockSpec((1, MEM_LEN, D_MODEL), lambda i: (i // tiles_per_seq, 0, 0)),
                  _const_spec((D_MODEL, D_MODEL))],
        out_specs=tok(D_MODEL),
        out_shape=jax.ShapeDtypeStruct((t_tokens, D_MODEL), F32),
        compiler_params=_params(("parallel",)),
        name="post",
    )(x2, ssd, att, prm["att_og"], prm["w_out1"], prm["w_out2"], prm["xatt_g"], prm["w_xq"],
      prm["xqg"], kt, vx, prm["w_xo"])


def _mlp_kernel(ff_chunk, h_ref, g_ref, w1_ref, w2_ref, y_out):
    h = h_ref[...]
    hm = _rms(h, g_ref[...]).astype(BF16)
    acc = h
    for c in range(D_FF // ff_chunk):
        sl = slice(c * ff_chunk, (c + 1) * ff_chunk)
        a = jnp.maximum(_dot(hm, w1_ref[:, sl]), 0.0)
        acc = acc + _dot((a * a).astype(BF16), w2_ref[sl, :])
    y_out[...] = acc


def _mlp(h, prm, tm):
    t_tokens = h.shape[0]
    tok = pl.BlockSpec((tm, D_MODEL), lambda i: (i, 0))
    return pl.pallas_call(
        functools.partial(_mlp_kernel, 1024),
        grid=(t_tokens // tm,),
        in_specs=[tok, _const_spec((1, D_MODEL)), _const_spec((D_MODEL, D_FF)),
                  _const_spec((D_FF, D_MODEL))],
        out_specs=tok,
        out_shape=jax.ShapeDtypeStruct((t_tokens, D_MODEL), F32),
        compiler_params=_params(("parallel",)),
        name="mlp",
    )(h, prm["mlp_g"], prm["w1"], prm["w2"])


def _rope_tables(seq):
    half = ROPE_DIMS // 2
    col = jnp.arange(3 * LANES)
    part = col // LANES
    dim = col % HEAD_DIM
    inv_freq = jnp.power(jnp.float32(ROPE_THETA), -(dim % half).astype(F32) / half)
    ang = jnp.arange(seq).astype(F32)[:, None] * inv_freq[None, :]
    sin = jnp.sin(ang)
    cos = jnp.where(dim < ROPE_DIMS, jnp.cos(ang), 1.0)
    sdn = jnp.where((dim >= half) & (dim < ROPE_DIMS), sin, 0.0)
    sup = jnp.where(dim < half, -sin, 0.0)
    return jnp.where(part == 0, cos, jnp.where(part == 1, sdn, sup))


def _prepare(seqs, mix_norm_g, w_in, conv_w, conv_b, ssd_A_log, ssd_dt_bias, ssd_D, ssd_norm_g,
             att_q_norm_g, att_k_norm_g, att_out_norm_g, w_out, xatt_norm_g, mem_norm_g, xatt_wq,
             xatt_wkv, xatt_q_norm_g, xatt_k_norm_g, xatt_wo, mlp_norm_g, mlp_w1, mlp_w2):
    row = lambda t: t.reshape(1, -1).astype(F32)
    pad_lanes = lambda t: jnp.pad(t, ((0, 0), (0, LANES - t.shape[1])))
    head_id = jnp.arange(ATT_WIDTH) // HEAD_DIM
    prm = {
        "mix_g": row(mix_norm_g),
        "w_z": w_in[:, OFF_Z:OFF_XBC].astype(BF16),
        "w_xbc": w_in[:, OFF_XBC:OFF_DT].astype(BF16),
        "w_dt": pad_lanes(w_in[:, OFF_DT:OFF_Q]).astype(BF16),
        "w_q": w_in[:, OFF_Q:OFF_K].astype(BF16),
        "w_k": w_in[:, OFF_K:OFF_V].astype(BF16),
        "w_v": w_in[:, OFF_V:IN_COLS].astype(BF16),
        "head_ones": (head_id[:, None] == head_id[None, :]).astype(BF16),
        "qg": jnp.tile(row(att_q_norm_g), (1, ATT_HEADS)),
        "kg": jnp.tile(row(att_k_norm_g), (1, ATT_HEADS)),
        "conv_w": conv_w.astype(F32),
        "conv_b": row(conv_b),
        "alog_row": pad_lanes(row(ssd_A_log)),
        "dtbias_row": pad_lanes(row(ssd_dt_bias)),
        "d_row": jnp.repeat(row(ssd_D), HEAD_DIM, axis=1),
        "ssd_ng": row(ssd_norm_g),
        "att_og": row(att_out_norm_g),
        "w_out1": w_out[:SSD_WIDTH].astype(BF16),
        "w_out2": w_out[SSD_WIDTH:].astype(BF16),
        "xatt_g": row(xatt_norm_g),
        "mem_g": row(mem_norm_g),
        "w_xq": xatt_wq.astype(BF16),
        "w_kv": xatt_wkv.astype(BF16),
        "xqg": row(xatt_q_norm_g),
        "xkg": row(xatt_k_norm_g),
        "w_xo": xatt_wo.astype(BF16),
        "mlp_g": row(mlp_norm_g),
        "w1": mlp_w1.astype(BF16),
        "w2": mlp_w2.astype(BF16),
        "rope": {seq: _rope_tables(seq) for seq in set(seqs)},
    }
    return prm


IN_TILE = 512
OUT_TILE = 1024


def _layer(x, mem, prm):
    bsz, seq, _ = x.shape
    x2 = x.reshape(bsz * seq, D_MODEL)
    outs = _inproj(x2, bsz, seq, prm, min(IN_TILE, seq))
    ssd = _ssd(outs[0], outs[1], prm, bsz, seq)
    att = _dilated_attention((outs[2:5], outs[5:8]), bsz, seq)
    kt, vx = _memkv(mem, prm)
    h = _post(x2, ssd, att, kt, vx, prm, seq, min(OUT_TILE, seq))
    y = _mlp(h, prm, min(OUT_TILE, seq))
    return y.reshape(bsz, seq, D_MODEL)


def kernel(x_prompt, x_sample, mem_prompt, mem_sample, mix_norm_g, w_in, conv_w, conv_b, ssd_A_log,
           ssd_dt_bias, ssd_D, ssd_norm_g, att_q_norm_g, att_k_norm_g, att_out_norm_g, w_out,
           xatt_norm_g, mem_norm_g, xatt_wq, xatt_wkv, xatt_q_norm_g, xatt_k_norm_g, xatt_wo,
           mlp_norm_g, mlp_w1, mlp_w2):
    weights = (mix_norm_g, w_in, conv_w, conv_b, ssd_A_log, ssd_dt_bias, ssd_D, ssd_norm_g,
               att_q_norm_g, att_k_norm_g, att_out_norm_g, w_out, xatt_norm_g, mem_norm_g, xatt_wq,
               xatt_wkv, xatt_q_norm_g, xatt_k_norm_g, xatt_wo, mlp_norm_g, mlp_w1, mlp_w2)
    assert all(w.shape[0] == 1 for w in weights), "single-layer stack expected"
    prm = _prepare((x_prompt.shape[1], x_sample.shape[1]), *(w[0] for w in weights))
    return (_layer(x_prompt, mem_prompt, prm), _layer(x_sample, mem_sample, prm))
```

```python
import functools
import math

import jax
import jax.numpy as jnp
from jax import lax
from jax.experimental import pallas as pl
from jax.experimental.pallas import tpu as pltpu

F32 = jnp.float32
BF16 = jnp.bfloat16

D_MODEL = 1024
HEAD_DIM = 64
SSD_HEADS = 8
SSD_WIDTH = SSD_HEADS * HEAD_DIM
SSD_GROUPS = 2
SSD_STATE = 128
SSD_CONV = 5
SSD_CHUNK = 128
ATT_HEADS = 8
ATT_WIDTH = ATT_HEADS * HEAD_DIM
DILATED_PATTERNS = ((128, 1), (512, 4), (2048, 16))
ROPE_DIMS = HEAD_DIM // 4
ROPE_THETA = 500000.0
MEM_LEN = 256
XATT_HEADS = 4
XATT_HEAD_DIM = D_MODEL // XATT_HEADS
D_FF = 4 * D_MODEL
EPS = 1e-6
NEG_INF = -1e30
CONV_CH = SSD_WIDTH + 2 * SSD_GROUPS * SSD_STATE
OFF_Z = 0
OFF_XBC = OFF_Z + SSD_WIDTH
OFF_DT = OFF_XBC + CONV_CH
OFF_Q = OFF_DT + 2 * SSD_HEADS
OFF_K = OFF_Q + ATT_WIDTH
OFF_V = OFF_K + ATT_WIDTH
IN_COLS = OFF_V + ATT_WIDTH

LANES = 128
BF16_ROWS = 16
VMEM_LIMIT = 56 * 1024 * 1024
SSD_CPS = 8
BAND_HALF = 64
ATT_Q = 128
ATT_KEYS = ATT_Q + 2 * BAND_HALF
ATT_BLK = 2048
ATT_GROUP = 16
LOG2E = math.log2(math.e)


def _dot(a, b):
    return jnp.dot(a, b, preferred_element_type=F32)


def _dot_nt(a, b):
    return lax.dot_general(a, b, (((1,), (1,)), ((), ())), preferred_element_type=F32)


def _const_spec(shape):
    zeros = (0,) * len(shape)
    return pl.BlockSpec(shape, lambda *_: zeros, pipeline_mode=pl.Buffered(1))


def _params(sem):
    return pltpu.CompilerParams(dimension_semantics=sem, vmem_limit_bytes=VMEM_LIMIT)


def _lane(shape):
    return lax.broadcasted_iota(jnp.int32, shape, len(shape) - 1)


def _rms(x, g):
    ms = jnp.mean(x * x, axis=-1, keepdims=True)
    return x * lax.rsqrt(ms + EPS) * g


def _silu(x):
    h = 0.5 * x
    return h + h * jnp.tanh(h)


def _emit_layouts(t, nat_out, c4_out, c16_out, slab_ref, slab2_ref):
    tm = t.shape[0]
    n4, n16 = tm // 4, tm // 16
    for p in range(ATT_WIDTH // LANES):
        tp = t[:, p * LANES:(p + 1) * LANES]
        nat_out[p] = tp.astype(BF16)
        slab_ref[p] = tp
        for r4 in range(4):
            c4 = slab_ref[p, pl.ds(r4, n4, stride=4), :]
            c4_out[0, p, r4] = c4.astype(BF16)
            slab2_ref[p, r4 * n4:(r4 + 1) * n4, :] = c4
        for r4 in range(4):
            for a in range(4):
                c16 = slab2_ref[p, pl.ds(r4 * n4 + a, n16, stride=4), :]
                c16_out[0, p, 4 * a + r4] = c16.astype(BF16)


def _inproj_kernel(x_ref, g_ref, wz_ref, wxbc_ref, wdt_ref, wq_ref, wk_ref, wv_ref, ones_ref,
                   qg_ref, kg_ref, rope_ref,
                   xz_out, dt_out, q1_out, q4_out, q16_out, kv1_out, kv4_out, kv16_out,
                   slab_ref, slab2_ref):
    xn = _rms(x_ref[...], g_ref[...]).astype(BF16)

    cos = jnp.concatenate([rope_ref[:, 0:LANES]] * 4, axis=1)
    sdn = jnp.concatenate([rope_ref[:, LANES:2 * LANES]] * 4, axis=1)
    sup = jnp.concatenate([rope_ref[:, 2 * LANES:]] * 4, axis=1)

    def norm_rope(w_ref, gain_ref, scale):
        t = _dot(xn, w_ref[...])
        ssq = _dot((t * t).astype(BF16), ones_ref[...])
        t = t * lax.rsqrt(ssq * (1.0 / HEAD_DIM) + EPS) * gain_ref[...]
        half = ROPE_DIMS // 2
        t = (t * cos + pltpu.roll(t, half, 1) * sdn
             + pltpu.roll(t, ATT_WIDTH - half, 1) * sup)
        return t * scale

    _emit_layouts(norm_rope(wq_ref, qg_ref, LOG2E / math.sqrt(HEAD_DIM)), q1_out, q4_out, q16_out,
                  slab_ref.at[0], slab2_ref.at[0])
    xz_out[:, CONV_CH:] = _dot(xn, wz_ref[...]).astype(BF16)
    _emit_layouts(norm_rope(wk_ref, kg_ref, 1.0), kv1_out.at[0], kv4_out.at[0], kv16_out.at[0],
                  slab_ref.at[1], slab2_ref.at[1])
    xz_out[:, :CONV_CH] = _dot(xn, wxbc_ref[...]).astype(BF16)
    _emit_layouts(_dot(xn, wv_ref[...]), kv1_out.at[1], kv4_out.at[1], kv16_out.at[1],
                  slab_ref.at[2], slab2_ref.at[2])
    dt_out[...] = _dot(xn, wdt_ref[...])


def _inproj(x2, bsz, seq, prm, tm):
    t_tokens = x2.shape[0]
    tiles_per_seq = seq // tm
    tok = lambda w: pl.BlockSpec((tm, w), lambda i: (i, 0))
    rope = pl.BlockSpec((tm, 3 * LANES), lambda i: (i % tiles_per_seq, 0))
    npair = ATT_WIDTH // LANES
    cls = lambda d: pl.BlockSpec((1, npair, d, tm // d, LANES),
                                 lambda i: (i // tiles_per_seq, 0, 0, i % tiles_per_seq, 0))
    cls2 = lambda d: pl.BlockSpec((2, 1, npair, d, tm // d, LANES),
                                  lambda i: (0, i // tiles_per_seq, 0, 0, i % tiles_per_seq, 0))
    cls_shape = lambda d: (bsz, npair, d, seq // d, LANES)
    out_shape = (
        jax.ShapeDtypeStruct((t_tokens, CONV_CH + SSD_WIDTH), BF16),
        jax.ShapeDtypeStruct((t_tokens, LANES), F32),
        jax.ShapeDtypeStruct((npair, t_tokens, LANES), BF16),
        jax.ShapeDtypeStruct(cls_shape(4), BF16),
        jax.ShapeDtypeStruct(cls_shape(16), BF16),
        jax.ShapeDtypeStruct((2, npair, t_tokens, LANES), BF16),
        jax.ShapeDtypeStruct((2,) + cls_shape(4), BF16),
        jax.ShapeDtypeStruct((2,) + cls_shape(16), BF16),
    )
    out_specs = (tok(CONV_CH + SSD_WIDTH), tok(LANES),
                 pl.BlockSpec((npair, tm, LANES), lambda i: (0, i, 0)), cls(4), cls(16),
                 pl.BlockSpec((2, npair, tm, LANES), lambda i: (0, 0, i, 0)), cls2(4), cls2(16))
    return pl.pallas_call(
        _inproj_kernel,
        grid=(t_tokens // tm,),
        in_specs=[
            tok(D_MODEL), _const_spec((1, D_MODEL)),
            _const_spec((D_MODEL, SSD_WIDTH)), _const_spec((D_MODEL, CONV_CH)),
            _const_spec((D_MODEL, LANES)), _const_spec((D_MODEL, ATT_WIDTH)),
            _const_spec((D_MODEL, ATT_WIDTH)), _const_spec((D_MODEL, ATT_WIDTH)),
            _const_spec((ATT_WIDTH, ATT_WIDTH)),
            _const_spec((1, ATT_WIDTH)), _const_spec((1, ATT_WIDTH)),
            rope,
        ],
        out_specs=out_specs,
        out_shape=out_shape,
        scratch_shapes=[pltpu.VMEM((3, ATT_WIDTH // LANES, tm, LANES), F32),
                        pltpu.VMEM((3, ATT_WIDTH // LANES, tm, LANES), F32)],
        compiler_params=_params(("parallel",)),
        name="inproj",
    )(x2, prm["mix_g"], prm["w_z"], prm["w_xbc"], prm["w_dt"], prm["w_q"], prm["w_k"], prm["w_v"],
      prm["head_ones"], prm["qg"], prm["kg"], prm["rope"][seq])


def _split3(a):
    a1 = a.astype(BF16)
    r1 = a - a1.astype(F32)
    a2 = r1.astype(BF16)
    a3 = (r1 - a2.astype(F32)).astype(BF16)
    return a1, a2, a3


def _dot_split(lhs_bf16, a):
    a1, a2, a3 = _split3(a)
    return _dot(lhs_bf16, a1) + _dot(lhs_bf16, a2) + _dot(lhs_bf16, a3)


def _ssd_kernel(nsteps, xbc_ref, xprev_ref, xnext_ref, z_ref, dt_ref, convw_ref, convb_ref, alog_ref,
                dtbias_ref, drow_ref, ng_ref, out_ref, ext_ref, carry_ref, prevb_ref, conv_ref):
    q = SSD_CHUNK
    nh = SSD_HEADS
    width = SSD_WIDTH // SSD_GROUPS
    pad = (SSD_CONV - 1) // 2
    s = pl.program_id(1)
    backward = s < nsteps
    step = jnp.where(backward, nsteps - 1 - s, s - nsteps)

    lane = _lane((q, LANES))
    row = lax.broadcasted_iota(jnp.int32, (q, LANES), 0)
    erow = lax.broadcasted_iota(jnp.int32, (LANES, 2 * SSD_WIDTH), 0)
    ecol = lax.broadcasted_iota(jnp.int32, (LANES, 2 * SSD_WIDTH), 1)
    expand = (erow == ecol // HEAD_DIM).astype(BF16)
    ltri = (lane <= row).astype(BF16)
    utri = (lane >= row).astype(BF16)

    def expand_rows(v, lo, hi):
        return _dot(v.astype(BF16), expand[:, lo:hi])

    def expand_row_exact(v, lo, hi):
        v1, v2, v3 = _split3(jnp.broadcast_to(v, (8, LANES)))
        e = expand[:, lo:hi]
        return (_dot(v1, e) + _dot(v2, e) + _dot(v3, e))[0:1]

    def step_decay(j):
        raw = dt_ref[0, j * q:(j + 1) * q, :] + dtbias_ref[...]
        softplus = jnp.maximum(raw, 0.0) + jnp.log(1.0 + jnp.exp(-jnp.abs(raw)))
        dt = jnp.where(lane < 2 * nh, softplus, 0.0)
        return dt, dt * (-jnp.exp(alog_ref[...]))

    def chunk_states(xs, bm16, w_full):
        xw = (xs * w_full).astype(BF16)
        parts = []
        for g in range(SSD_GROUPS):
            bt = bm16[:, g * SSD_STATE:(g + 1) * SSD_STATE].astype(F32).T.astype(BF16)
            parts.append(_dot(bt, xw[:, g * width:(g + 1) * width]))
        return jnp.concatenate(parts, axis=1)

    @pl.when(s == 0)
    def _():
        carry_ref[...] = jnp.zeros_like(carry_ref)

    @pl.when(s == nsteps)
    def _():
        carry_ref[...] = jnp.zeros_like(carry_ref)

    @pl.when(backward)
    def _():
        rows = SSD_CPS * q
        prev = xprev_ref[0]
        nxt = xnext_ref[0]
        ext_ref[0:BF16_ROWS, :] = jnp.where(step > 0, prev, jnp.zeros_like(prev))
        ext_ref[BF16_ROWS:BF16_ROWS + rows, :] = xbc_ref[0]
        ext_ref[BF16_ROWS + rows:, :] = jnp.where(step < nsteps - 1, nxt, jnp.zeros_like(nxt))

        win_rows = q + 2 * BF16_ROWS
        srow = lax.broadcasted_iota(jnp.int32, (q, win_rows), 0)
        scol = lax.broadcasted_iota(jnp.int32, (q, win_rows), 1)
        shifts = {k: (scol == srow + BF16_ROWS + k - pad).astype(BF16)
                  for k in range(SSD_CONV) if k != pad}

        bmask = (lane >= nh) & (lane < 2 * nh)

        def decay_terms(j):
            dt, a = step_decay(j)
            sb = _dot_split(utri, a)
            wb = jnp.where(bmask, jnp.exp(sb[0:1, :] - sb) * dt, 0.0)
            decay = expand_row_exact(jnp.where(bmask[0:1], jnp.exp(sb[0:1, :]), 0.0),
                                     SSD_WIDTH, 2 * SSD_WIDTH)
            return expand_rows(wb, SSD_WIDTH, 2 * SSD_WIDTH), decay

        def conv_dots(j):
            win = ext_ref[j * q:j * q + win_rows, :]
            return win, {k: _dot(sh, win) for k, sh in shifts.items()}

        def conv_finish(j, win, dots):
            acc = convb_ref[...] + win[BF16_ROWS:BF16_ROWS + q].astype(F32) * convw_ref[pad:pad + 1, :]
            for k, d in dots.items():
                acc = acc + d * convw_ref[k:k + 1, :]
            conv_ref[step * SSD_CPS + j] = _silu(acc).astype(BF16)

        pending = conv_dots(0)
        terms = []
        for j in range(SSD_CPS):
            ahead = conv_dots(j + 1) if j + 1 < SSD_CPS else None
            terms.append(decay_terms(j))
            conv_finish(j, *pending)
            pending = ahead

        states = []
        for j in range(SSD_CPS):
            xb16 = conv_ref[step * SSD_CPS + j]
            states.append(chunk_states(xb16[:, :SSD_WIDTH].astype(F32),
                                       xb16[:, SSD_WIDTH:SSD_WIDTH + SSD_GROUPS * SSD_STATE],
                                       terms[j][0]))
        carry = carry_ref[...]
        for j in reversed(range(SSD_CPS)):
            prevb_ref[step * SSD_CPS + j] = carry.astype(BF16)
            carry = carry * terms[j][1] + states[j]
        carry_ref[...] = carry

    @pl.when(jnp.logical_not(backward))
    def _():
        ii = lax.broadcasted_iota(jnp.int32, (q, q), 0)
        jj = lax.broadcasted_iota(jnp.int32, (q, q), 1)
        lane_p = _lane((q, LANES))
        fmask = lane < nh
        pre = []
        prevs = []
        carry = carry_ref[...]

        def decay_terms(j):
            dt, a = step_decay(j)
            cf = _dot_split(ltri, a)
            sb = _dot_split(utri, a)
            ldt = jnp.where(lane < 2 * nh, jnp.log2(dt), 0.0)
            pack = jnp.where(fmask, cf * LOG2E,
                             jnp.where(lane < 2 * nh, sb * LOG2E, pltpu.roll(ldt, 2 * nh, 1)))
            pack_t = pack.T
            dsum = dt + pltpu.roll(dt, LANES - nh, 1)
            dsum_t = jnp.where(fmask, jnp.log2(dsum), 0.0).T
            edge = expand_rows(jnp.where(lane < 2 * nh, jnp.exp2(pack), 0.0), 0, 2 * SSD_WIDTH)
            wf = jnp.where(fmask, jnp.exp(cf[q - 1:q, :] - cf) * dt, 0.0)
            decay = expand_row_exact(jnp.where(fmask[0:1], jnp.exp(cf[q - 1:q, :]), 0.0),
                                     0, SSD_WIDTH)
            pre.append((pack, pack_t, dsum_t, edge, expand_rows(wf, 0, SSD_WIDTH), decay))

        for j in range(SSD_CPS):
            decay_terms(j)
        for j in range(SSD_CPS):
            xb16 = conv_ref[step * SSD_CPS + j]
            st = chunk_states(xb16[:, :SSD_WIDTH].astype(F32),
                              xb16[:, SSD_WIDTH:SSD_WIDTH + SSD_GROUPS * SSD_STATE], pre[j][4])
            prevs.append(carry.astype(BF16))
            carry = carry * pre[j][5] + st
        carry_ref[...] = carry

        for j in range(SSD_CPS):
            c = step * SSD_CPS + j
            pack, pack_t, dsum_t, edge = pre[j][:4]
            xb16 = conv_ref[c]
            xs16 = xb16[:, :SSD_WIDTH]
            xs = xs16.astype(F32)
            bm16 = xb16[:, SSD_WIDTH:SSD_WIDTH + SSD_GROUPS * SSD_STATE]
            cm16 = xb16[:, SSD_WIDTH + SSD_GROUPS * SSD_STATE:]

            cb = [_dot_nt(cm16[:, g * SSD_STATE:(g + 1) * SSD_STATE],
                          bm16[:, g * SSD_STATE:(g + 1) * SSD_STATE]) for g in range(SSD_GROUPS)]
            mats = []
            for h in range(nh):
                low_row = pack_t[h:h + 1, :] - pack_t[2 * nh + h:2 * nh + h + 1, :]
                up_row = pack_t[nh + h:nh + h + 1, :] - pack_t[3 * nh + h:3 * nh + h + 1, :]
                expo = jnp.where(jj < ii, pack[:, h:h + 1] - low_row,
                                 jnp.where(jj > ii, pack[:, nh + h:nh + h + 1] - up_row,
                                           dsum_t[h:h + 1, :]))
                mats.append((cb[h // (nh // SSD_GROUPS)] * jnp.exp2(expo)).astype(BF16))

            y_parts = []
            for p in range(nh // 2):
                xp = xs16[:, p * LANES:(p + 1) * LANES]
                zero = jnp.zeros_like(xp)
                rhs = jnp.concatenate([jnp.where(lane_p < HEAD_DIM, xp, zero),
                                       jnp.where(lane_p >= HEAD_DIM, xp, zero)], axis=0)
                lhs = jnp.concatenate([mats[2 * p], mats[2 * p + 1]], axis=1)
                y_parts.append(_dot(lhs, rhs))
            y = jnp.concatenate(y_parts, axis=1)

            prev_f = prevs[j]
            prev_b = prevb_ref[c]
            off_parts = []
            for g in range(SSD_GROUPS):
                st = jnp.concatenate([prev_f[:, g * width:(g + 1) * width],
                                      prev_b[:, g * width:(g + 1) * width]], axis=1)
                res = _dot(cm16[:, g * SSD_STATE:(g + 1) * SSD_STATE], st)
                off_parts.append(res[:, :width] * edge[:, g * width:(g + 1) * width]
                                 + res[:, width:] * edge[:, SSD_WIDTH + g * width:
                                                         SSD_WIDTH + (g + 1) * width])
            y = y + jnp.concatenate(off_parts, axis=1) + drow_ref[...] * xs

            y = y * _silu(z_ref[0, j * q:(j + 1) * q, :].astype(F32))
            out_ref[0, j * q:(j + 1) * q, :] = _rms(y, ng_ref[...]).astype(BF16)


def _ssd(xz, dt, prm, bsz, seq):
    q = SSD_CHUNK
    rows = SSD_CPS * q
    assert seq % rows == 0
    nsteps = seq // rows
    nc = seq // q
    halo_blocks = rows // BF16_ROWS
    conv_step = lambda s: jnp.where(s < nsteps, nsteps - 1 - s, 0)
    both_step = lambda s: jnp.where(s < nsteps, nsteps - 1 - s, s - nsteps)
    fwd_step = lambda s: jnp.maximum(s - nsteps, 0)
    xz3 = xz.reshape(bsz, seq, CONV_CH + SSD_WIDTH)
    gate_blk = CONV_CH // SSD_WIDTH
    dt3 = dt.reshape(bsz, seq, LANES)
    in_specs = [
        pl.BlockSpec((1, rows, CONV_CH), lambda b, s: (b, conv_step(s), 0)),
        pl.BlockSpec((1, BF16_ROWS, CONV_CH),
                     lambda b, s: (b, jnp.maximum(conv_step(s) * halo_blocks - 1, 0), 0)),
        pl.BlockSpec((1, BF16_ROWS, CONV_CH),
                     lambda b, s: (b, jnp.minimum((conv_step(s) + 1) * halo_blocks,
                                                  seq // BF16_ROWS - 1), 0)),
        pl.BlockSpec((1, rows, SSD_WIDTH), lambda b, s: (b, fwd_step(s), gate_blk)),
        pl.BlockSpec((1, rows, LANES), lambda b, s: (b, both_step(s), 0)),
        _const_spec((SSD_CONV, CONV_CH)), _const_spec((1, CONV_CH)),
        _const_spec((1, LANES)), _const_spec((1, LANES)),
        _const_spec((1, SSD_WIDTH)), _const_spec((1, SSD_WIDTH)),
    ]
    out_spec = pl.BlockSpec((1, rows, SSD_WIDTH), lambda b, s: (b, fwd_step(s), 0))
    out = pl.pallas_call(
        functools.partial(_ssd_kernel, nsteps),
        grid=(bsz, 2 * nsteps),
        in_specs=in_specs,
        out_specs=out_spec,
        out_shape=jax.ShapeDtypeStruct((bsz, seq, SSD_WIDTH), BF16),
        scratch_shapes=[
            pltpu.VMEM((rows + 2 * BF16_ROWS, CONV_CH), BF16),
            pltpu.VMEM((SSD_STATE, SSD_WIDTH), F32),
            pltpu.VMEM((nc, SSD_STATE, SSD_WIDTH), BF16),
            pltpu.VMEM((nc, q, CONV_CH), BF16),
        ],
        compiler_params=_params(("parallel", "arbitrary")),
        name="ssd",
    )(xz3, xz3, xz3, xz3, dt3, prm["conv_w"], prm["conv_b"], prm["alog_row"], prm["dtbias_row"],
      prm["d_row"], prm["ssd_ng"])
    return out.reshape(bsz * seq, SSD_WIDTH)


def _attn_unit(q_tile, kwin_ref, vwin_ref, row0, bias, m_old):
    low = _lane((ATT_Q, LANES)) < HEAD_DIM
    zero = jnp.zeros_like(q_tile)
    lhs = jnp.concatenate([jnp.where(low, q_tile, zero), jnp.where(low, zero, q_tile)], axis=0)
    sc = _dot_nt(lhs, kwin_ref[pl.ds(row0, ATT_KEYS), :]) + bias
    part = jnp.maximum(sc[:, :LANES], sc[:, LANES:])
    if m_old is not None:
        ninf = jnp.full_like(m_old, NEG_INF)
        part = jnp.maximum(part, jnp.concatenate([jnp.where(low, m_old, ninf),
                                                  jnp.where(low, ninf, m_old)], axis=0))
    mx = jnp.max(part, axis=-1, keepdims=True)
    pr = jnp.exp2(sc - mx).astype(BF16)
    ones = jnp.ones((ATT_KEYS, LANES), BF16)
    vcat = jnp.concatenate([vwin_ref[pl.ds(row0, ATT_KEYS), :], ones], axis=1)
    res = _dot(pr, vcat)
    num = jnp.where(low, res[:ATT_Q, :LANES], res[ATT_Q:, :LANES])
    den = jnp.where(low, res[:ATT_Q, LANES:], res[ATT_Q:, LANES:])
    mxp = jnp.where(low, jnp.broadcast_to(mx[:ATT_Q], (ATT_Q, LANES)),
                    jnp.broadcast_to(mx[ATT_Q:], (ATT_Q, LANES)))
    return num, den, mxp


def _attn_kernel(nblk, q1, kv1c, kv1p, kv1n, q4, kv4c, kv4p, kv4n, q16, kv16c, kv16p, kv16n,
                 o_out, kwin, vwin, acc_ref, den_ref, max_ref, bias_ref):
    q1, q4, q16, o_out = (r.at[0] for r in (q1, q4, q16, o_out))
    k1c, k1p, k1n, v1c, v1p, v1n = (r.at[i].at[0] for i in range(2) for r in (kv1c, kv1p, kv1n))
    k4c, k4p, k4n, v4c, v4p, v4n = (r.at[i].at[0] for i in range(2) for r in (kv4c, kv4p, kv4n))
    k16c, k16p, k16n, v16c, v16p, v16n = (r.at[i].at[0] for i in range(2)
                                          for r in (kv16c, kv16p, kv16n))
    n = pl.program_id(2)
    half = BAND_HALF
    rows2 = 2 * ATT_Q
    qi = lax.broadcasted_iota(jnp.int32, (rows2, ATT_KEYS), 0) % ATT_Q
    kt = lax.broadcasted_iota(jnp.int32, (rows2, ATT_KEYS), 1)
    band = jnp.where(jnp.abs(kt - half - qi) <= half, 0.0, NEG_INF)
    bias_ref[0] = band
    bias_ref[1] = jnp.where(kt < half, NEG_INF, band)
    bias_ref[2] = jnp.where(kt >= half + ATT_Q, NEG_INF, band)

    def bias_for(first_sub, last_sub):
        idx = jnp.where((n == 0) & first_sub, 1, jnp.where((n == nblk - 1) & last_sub, 2, 0))
        return bias_ref[idx]

    def fill(win, base, prev, cur, nxt, rows):
        win[pl.ds(base, half), :] = prev
        win[pl.ds(base + half, rows), :] = cur
        win[pl.ds(base + half + rows, half), :] = nxt

    def load_state(rows):
        return acc_ref[rows, :], den_ref[rows, :], max_ref[rows, :]

    def merge(old, num, den, mxp):
        scale = jnp.exp2(old[2] - mxp)
        return old[0] * scale + num, old[1] * scale + den, mxp

    def store_state(rows, new):
        acc_ref[rows, :] = new[0]
        den_ref[rows, :] = new[1]
        max_ref[rows, :] = new[2]

    span16 = ATT_KEYS

    def fill16(r, carry):
        base = pl.multiple_of(r * span16, span16)
        fill(kwin, base, k16p[0, r], k16c[0, r], k16n[0, r], ATT_Q)
        fill(vwin, base, v16p[0, r], v16c[0, r], v16n[0, r], ATT_Q)
        return carry

    lax.fori_loop(0, 16, fill16, 0)

    def group16(g, carry):
        new = []
        for u in range(ATT_GROUP):
            r = g * ATT_GROUP + u
            new.append(_attn_unit(q16[0, r], kwin, vwin, pl.multiple_of(r * span16, span16),
                                  bias_for(True, True), None))
        for u in range(ATT_GROUP):
            store_state(pl.ds(g * ATT_GROUP + u, ATT_Q, stride=16), new[u])
        return carry

    lax.fori_loop(0, 16 // ATT_GROUP, group16, 0)

    rows4 = ATT_BLK // 4
    nsub4 = rows4 // ATT_Q
    span4 = rows4 + 2 * half
    for r in range(4):
        fill(kwin, r * span4, k4p[0, r], k4c[0, r], k4n[0, r], rows4)
        fill(vwin, r * span4, v4p[0, r], v4c[0, r], v4n[0, r], rows4)

    ncls = ATT_GROUP // nsub4

    def group4(g, carry):
        units = [(g * ncls + u, j) for u in range(ncls) for j in range(nsub4)]
        rows = [pl.ds(r + 4 * ATT_Q * j, ATT_Q, stride=4) for r, j in units]
        old = [load_state(rw) for rw in rows]
        new = []
        for (r, j), prev in zip(units, old):
            unit = _attn_unit(q4[0, r, j * ATT_Q:(j + 1) * ATT_Q, :], kwin, vwin,
                              pl.multiple_of(r * span4, ATT_Q) + j * ATT_Q,
                              bias_for(j == 0, j == nsub4 - 1), prev[2])
            new.append(merge(prev, *unit))
        for rw, nw in zip(rows, new):
            store_state(rw, nw)
        return carry

    lax.fori_loop(0, 4 // ncls, group4, 0)

    fill(kwin, 0, k1p[0], k1c[0], k1n[0], ATT_BLK)
    fill(vwin, 0, v1p[0], v1c[0], v1n[0], ATT_BLK)
    nsub1 = ATT_BLK // ATT_Q

    def group1(g, carry):
        for u in range(ATT_GROUP):
            j = g * ATT_GROUP + u
            r0 = pl.multiple_of(j * ATT_Q, ATT_Q)
            rows = pl.ds(r0, ATT_Q)
            old = load_state(rows)
            unit = _attn_unit(q1[0, rows, :], kwin, vwin, r0, bias_for(j == 0, j == nsub1 - 1),
                              old[2])
            acc_new, den_new, _ = merge(old, *unit)
            o_out[0, rows, :] = (acc_new / den_new).astype(BF16)
        return carry

    lax.fori_loop(0, nsub1 // ATT_GROUP, group1, 0)


def _dilated_attention(qkv, bsz, seq):
    (q1, q4, q16), (kv1, kv4, kv16) = qkv
    assert tuple(d for _, d in DILATED_PATTERNS) == (1, 4, 16)
    assert all(w // (2 * d) == BAND_HALF for w, d in DILATED_PATTERNS)
    assert seq % ATT_BLK == 0 and seq // 16 >= 2 * ATT_Q
    nblk = seq // ATT_BLK
    npair = ATT_WIDTH // LANES
    def specs(dil):
        rows = ATT_BLK // dil
        per_blk = rows // BAND_HALF
        last = seq // dil // BAND_HALF - 1
        before = lambda n: jnp.maximum(n * per_blk - 1, 0)
        after = lambda n: jnp.minimum((n + 1) * per_blk, last)
        if dil == 1:
            blk = lambda r: (1, 1, r, LANES)
            idx = lambda b, p, m: (p, b, m, 0)
        else:
            blk = lambda r: (1, 1, dil, r, LANES)
            idx = lambda b, p, m: (b, p, 0, m, 0)
        return [
            pl.BlockSpec(blk(rows), lambda b, p, n: idx(b, p, n)),
            pl.BlockSpec((2,) + blk(rows), lambda b, p, n: (0,) + idx(b, p, n)),
            pl.BlockSpec((2,) + blk(BAND_HALF), lambda b, p, n: (0,) + idx(b, p, before(n))),
            pl.BlockSpec((2,) + blk(BAND_HALF), lambda b, p, n: (0,) + idx(b, p, after(n))),
        ]

    q1 = q1.reshape(npair, bsz, seq, LANES)
    kv1 = kv1.reshape(2, npair, bsz, seq, LANES)
    args = [q1] + [kv1] * 3 + [q4] + [kv4] * 3 + [q16] + [kv16] * 3
    win_rows = 16 * ATT_KEYS
    out = pl.pallas_call(
        functools.partial(_attn_kernel, nblk),
        grid=(bsz, npair, nblk),
        in_specs=specs(1) + specs(4) + specs(16),
        out_specs=pl.BlockSpec((1, 1, ATT_BLK, LANES), lambda b, p, n: (b, p, n, 0)),
        out_shape=jax.ShapeDtypeStruct((bsz, npair, seq, LANES), BF16),
        scratch_shapes=[pltpu.VMEM((win_rows, LANES), BF16),
                        pltpu.VMEM((win_rows, LANES), BF16),
                        pltpu.VMEM((ATT_BLK, LANES), F32),
                        pltpu.VMEM((ATT_BLK, LANES), F32),
                        pltpu.VMEM((ATT_BLK, LANES), F32),
                        pltpu.VMEM((3, 2 * ATT_Q, ATT_KEYS), F32)],
        compiler_params=_params(("parallel", "parallel", "parallel")),
        name="attn",
    )(*args)
    return out


def _memkv_kernel(mem_ref, g_ref, wkv_ref, kg_ref, kt_out, v_out):
    mn = _rms(mem_ref[0], g_ref[...]).astype(BF16)
    kv = _dot(mn, wkv_ref[...])
    scale = 1.0 / math.sqrt(XATT_HEAD_DIM)
    for h in range(XATT_HEADS):
        kh = kv[:, h * XATT_HEAD_DIM:(h + 1) * XATT_HEAD_DIM]
        kh = _rms(kh, kg_ref[...]) * scale
        kt_out[0, h * XATT_HEAD_DIM:(h + 1) * XATT_HEAD_DIM, :] = kh.T.astype(BF16)
    v_out[0] = kv[:, D_MODEL:].astype(BF16)


def _memkv(mem, prm):
    bsz = mem.shape[0]
    return pl.pallas_call(
        _memkv_kernel,
        grid=(bsz,),
        in_specs=[pl.BlockSpec((1, MEM_LEN, D_MODEL), lambda b: (b, 0, 0)),
                  _const_spec((1, D_MODEL)), _const_spec((D_MODEL, 2 * D_MODEL)),
                  _const_spec((1, XATT_HEAD_DIM))],
        out_specs=(pl.BlockSpec((1, D_MODEL, MEM_LEN), lambda b: (b, 0, 0)),
                   pl.BlockSpec((1, MEM_LEN, D_MODEL), lambda b: (b, 0, 0))),
        out_shape=(jax.ShapeDtypeStruct((bsz, D_MODEL, MEM_LEN), BF16),
                   jax.ShapeDtypeStruct((bsz, MEM_LEN, D_MODEL), BF16)),
        compiler_params=_params(("parallel",)),
        name="memkv",
    )(mem, prm["mem_g"], prm["w_kv"], prm["xkg"])


def _post_kernel(x_ref, ssd_ref, att_ref, og_ref, wo1_ref, wo2_ref, g_ref, wq_ref, qg_ref, kt_ref,
                 v_ref, wo_ref, h_out):
    att = jnp.concatenate([att_ref[0, p] for p in range(ATT_WIDTH // LANES)], axis=1)
    att = _rms(att.astype(F32), og_ref[...]).astype(BF16)
    h = x_ref[...] + _dot(ssd_ref[...], wo1_ref[...]) + _dot(att, wo2_ref[...])
    hn = _rms(h, g_ref[...]).astype(BF16)
    qf = _dot(hn, wq_ref[...])
    outs = []
    for hd in range(XATT_HEADS):
        sl = slice(hd * XATT_HEAD_DIM, (hd + 1) * XATT_HEAD_DIM)
        qh = _rms(qf[:, sl], qg_ref[...]).astype(BF16)
        sc = _dot(qh, kt_ref[0, sl, :])
        mx = jnp.max(sc, axis=-1, keepdims=True)
        pr = jnp.exp(sc - mx)
        den = jnp.sum(pr, axis=-1, keepdims=True)
        outs.append(_dot(pr.astype(BF16), v_ref[0, :, sl]) / den)
    o = jnp.concatenate(outs, axis=1).astype(BF16)
    h_out[...] = h + _dot(o, wo_ref[...])


def _post(x2, ssd, att, kt, vx, prm, seq, tm):
    t_tokens = x2.shape[0]
    tiles_per_seq = seq // tm
    tok = lambda w: pl.BlockSpec((tm, w), lambda i: (i, 0))
    half = D_MODEL // 2
    return pl.pallas_call(
        _post_kernel,
        grid=(t_tokens // tm,),
        in_specs=[tok(D_MODEL), tok(SSD_WIDTH),
                  pl.BlockSpec((1, ATT_WIDTH // LANES, tm, LANES),
                               lambda i: (i // tiles_per_seq, 0, i % tiles_per_seq, 0)),
                  _const_spec((1, ATT_WIDTH)),
                  _const_spec((half, D_MODEL)), _const_spec((half, D_MODEL)),
                  _const_spec((1, D_MODEL)), _const_spec((D_MODEL, D_MODEL)),
                  _const_spec((1, XATT_HEAD_DIM)),
                  pl.BlockSpec((1, D_MODEL, MEM_LEN), lambda i: (i // tiles_per_seq, 0, 0)),
                  pl.BlockSpec((1, MEM_LEN, D_MODEL), lambda i: (i // tiles_per_seq, 0, 0)),
                  _const_spec((D_MODEL, D_MODEL))],
        out_specs=tok(D_MODEL),
        out_shape=jax.ShapeDtypeStruct((t_tokens, D_MODEL), F32),
        compiler_params=_params(("parallel",)),
        name="post",
    )(x2, ssd, att, prm["att_og"], prm["w_out1"], prm["w_out2"], prm["xatt_g"], prm["w_xq"],
      prm["xqg"], kt, vx, prm["w_xo"])


def _mlp_kernel(ff_chunk, h_ref, g_ref, w1_ref, w2_ref, y_out):
    h = h_ref[...]
    hm = _rms(h, g_ref[...]).astype(BF16)
    acc = h
    for c in range(D_FF // ff_chunk):
        sl = slice(c * ff_chunk, (c + 1) * ff_chunk)
        a = jnp.maximum(_dot(hm, w1_ref[:, sl]), 0.0)
        acc = acc + _dot((a * a).astype(BF16), w2_ref[sl, :])
    y_out[...] = acc


def _mlp(h, prm, tm):
    t_tokens = h.shape[0]
    tok = pl.BlockSpec((tm, D_MODEL), lambda i: (i, 0))
    return pl.pallas_call(
        functools.partial(_mlp_kernel, 1024),
        grid=(t_tokens // tm,),
        in_specs=[tok, _const_spec((1, D_MODEL)), _const_spec((D_MODEL, D_FF)),
                  _const_spec((D_FF, D_MODEL))],
        out_specs=tok,
        out_shape=jax.ShapeDtypeStruct((t_tokens, D_MODEL), F32),
        compiler_params=_params(("parallel",)),
        name="mlp",
    )(h, prm["mlp_g"], prm["w1"], prm["w2"])


def _rope_tables(seq):
    half = ROPE_DIMS // 2
    dim = jnp.arange(LANES) % HEAD_DIM
    inv_freq = jnp.power(jnp.float32(ROPE_THETA), -(dim % half).astype(F32) / half)
    ang = jnp.arange(seq).astype(F32)[:, None] * inv_freq[None, :]
    sin = jnp.sin(ang)
    cos = jnp.where(dim < ROPE_DIMS, jnp.cos(ang), 1.0)
    sdn = jnp.where((dim >= half) & (dim < ROPE_DIMS), sin, 0.0)
    sup = jnp.where(dim < half, -sin, 0.0)
    return jnp.concatenate([cos, sdn, sup], axis=1)


def _prepare(seqs, mix_norm_g, w_in, conv_w, conv_b, ssd_A_log, ssd_dt_bias, ssd_D, ssd_norm_g,
             att_q_norm_g, att_k_norm_g, att_out_norm_g, w_out, xatt_norm_g, mem_norm_g, xatt_wq,
             xatt_wkv, xatt_q_norm_g, xatt_k_norm_g, xatt_wo, mlp_norm_g, mlp_w1, mlp_w2):
    row = lambda t: t.reshape(1, -1).astype(F32)
    pad_lanes = lambda t: jnp.pad(t, ((0, 0), (0, LANES - t.shape[1])))
    head_id = jnp.arange(ATT_WIDTH) // HEAD_DIM
    prm = {
        "mix_g": row(mix_norm_g),
        "w_z": w_in[:, OFF_Z:OFF_XBC].astype(BF16),
        "w_xbc": w_in[:, OFF_XBC:OFF_DT].astype(BF16),
        "w_dt": pad_lanes(w_in[:, OFF_DT:OFF_Q]).astype(BF16),
        "w_q": w_in[:, OFF_Q:OFF_K].astype(BF16),
        "w_k": w_in[:, OFF_K:OFF_V].astype(BF16),
        "w_v": w_in[:, OFF_V:IN_COLS].astype(BF16),
        "head_ones": (head_id[:, None] == head_id[None, :]).astype(BF16),
        "qg": jnp.tile(row(att_q_norm_g), (1, ATT_HEADS)),
        "kg": jnp.tile(row(att_k_norm_g), (1, ATT_HEADS)),
        "conv_w": conv_w.astype(F32),
        "conv_b": row(conv_b),
        "alog_row": pad_lanes(row(ssd_A_log)),
        "dtbias_row": pad_lanes(row(ssd_dt_bias)),
        "d_row": jnp.repeat(row(ssd_D), HEAD_DIM, axis=1),
        "ssd_ng": row(ssd_norm_g),
        "att_og": row(att_out_norm_g),
        "w_out1": w_out[:SSD_WIDTH].astype(BF16),
        "w_out2": w_out[SSD_WIDTH:].astype(BF16),
        "xatt_g": row(xatt_norm_g),
        "mem_g": row(mem_norm_g),
        "w_xq": xatt_wq.astype(BF16),
        "w_kv": xatt_wkv.astype(BF16),
        "xqg": row(xatt_q_norm_g),
        "xkg": row(xatt_k_norm_g),
        "w_xo": xatt_wo.astype(BF16),
        "mlp_g": row(mlp_norm_g),
        "w1": mlp_w1.astype(BF16),
        "w2": mlp_w2.astype(BF16),
        "rope": {seq: _rope_tables(seq) for seq in set(seqs)},
    }
    return prm


IN_TILE = 512
OUT_TILE = 1024


def _layer(x, mem, prm):
    bsz, seq, _ = x.shape
    x2 = x.reshape(bsz * seq, D_MODEL)
    outs = _inproj(x2, bsz, seq, prm, min(IN_TILE, seq))
    ssd = _ssd(outs[0], outs[1], prm, bsz, seq)
    att = _dilated_attention((outs[2:5], outs[5:8]), bsz, seq)
    kt, vx = _memkv(mem, prm)
    h = _post(x2, ssd, att, kt, vx, prm, seq, min(OUT_TILE, seq))
    y = _mlp(h, prm, min(OUT_TILE, seq))
    return y.reshape(bsz, seq, D_MODEL)


def kernel(x_prompt, x_sample, mem_prompt, mem_sample, mix_norm_g, w_in, conv_w, conv_b, ssd_A_log,
           ssd_dt_bias, ssd_D, ssd_norm_g, att_q_norm_g, att_k_norm_g, att_out_norm_g, w_out,
           xatt_norm_g, mem_norm_g, xatt_wq, xatt_wkv, xatt_q_norm_g, xatt_k_norm_g, xatt_wo,
           mlp_norm_g, mlp_w1, mlp_w2):
    weights = (mix_norm_g, w_in, conv_w, conv_b, ssd_A_log, ssd_dt_bias, ssd_D, ssd_norm_g,
               att_q_norm_g, att_k_norm_g, att_out_norm_g, w_out, xatt_norm_g, mem_norm_g, xatt_wq,
               xatt_wkv, xatt_q_norm_g, xatt_k_norm_g, xatt_wo, mlp_norm_g, mlp_w1, mlp_w2)
    assert all(w.shape[0] == 1 for w in weights), "single-layer stack expected"
    prm = _prepare((x_prompt.shape[1], x_sample.shape[1]), *(w[0] for w in weights))
    return (_layer(x_prompt, mem_prompt, prm), _layer(x_sample, mem_sample, prm))
```

```python
import functools
import math

import jax
import jax.numpy as jnp
from jax import lax
from jax.experimental import pallas as pl
from jax.experimental.pallas import tpu as pltpu

F32 = jnp.float32
BF16 = jnp.bfloat16

D_MODEL = 1024
HEAD_DIM = 64
SSD_HEADS = 8
SSD_WIDTH = SSD_HEADS * HEAD_DIM
SSD_GROUPS = 2
SSD_STATE = 128
SSD_CONV = 5
SSD_CHUNK = 128
ATT_HEADS = 8
ATT_WIDTH = ATT_HEADS * HEAD_DIM
DILATED_PATTERNS = ((128, 1), (512, 4), (2048, 16))
ROPE_DIMS = HEAD_DIM // 4
ROPE_THETA = 500000.0
MEM_LEN = 256
XATT_HEADS = 4
XATT_HEAD_DIM = D_MODEL // XATT_HEADS
D_FF = 4 * D_MODEL
EPS = 1e-6
NEG_INF = -1e30
CONV_CH = SSD_WIDTH + 2 * SSD_GROUPS * SSD_STATE
OFF_Z = 0
OFF_XBC = OFF_Z + SSD_WIDTH
OFF_DT = OFF_XBC + CONV_CH
OFF_Q = OFF_DT + 2 * SSD_HEADS
OFF_K = OFF_Q + ATT_WIDTH
OFF_V = OFF_K + ATT_WIDTH
IN_COLS = OFF_V + ATT_WIDTH

LANES = 128
BF16_ROWS = 16
VMEM_LIMIT = 56 * 1024 * 1024
SSD_CPS = 8
BAND_HALF = 64
ATT_Q = 128
ATT_KEYS = ATT_Q + 2 * BAND_HALF
ATT_BLK = 2048
ATT_GROUP = 16
LOG2E = math.log2(math.e)


def _dot(a, b):
    return jnp.dot(a, b, preferred_element_type=F32)


def _dot_nt(a, b):
    return lax.dot_general(a, b, (((1,), (1,)), ((), ())), preferred_element_type=F32)


def _const_spec(shape):
    zeros = (0,) * len(shape)
    return pl.BlockSpec(shape, lambda *_: zeros, pipeline_mode=pl.Buffered(1))


def _params(sem):
    return pltpu.CompilerParams(dimension_semantics=sem, vmem_limit_bytes=VMEM_LIMIT)


def _lane(shape):
    return lax.broadcasted_iota(jnp.int32, shape, len(shape) - 1)


def _rms(x, g):
    ms = jnp.mean(x * x, axis=-1, keepdims=True)
    return x * lax.rsqrt(ms + EPS) * g


def _silu(x):
    h = 0.5 * x
    return h + h * jnp.tanh(h)


def _emit_layouts(t, nat_out, c4_out, c16_out, slab_ref, slab2_ref):
    tm = t.shape[0]
    n4, n16 = tm // 4, tm // 16
    for p in range(ATT_WIDTH // LANES):
        tp = t[:, p * LANES:(p + 1) * LANES]
        nat_out[p] = tp.astype(BF16)
        slab_ref[p] = tp
        for r4 in range(4):
            c4 = slab_ref[p, pl.ds(r4, n4, stride=4), :]
            c4_out[0, p, r4] = c4.astype(BF16)
            slab2_ref[p, r4 * n4:(r4 + 1) * n4, :] = c4
        for r4 in range(4):
            for a in range(4):
                c16 = slab2_ref[p, pl.ds(r4 * n4 + a, n16, stride=4), :]
                c16_out[0, p, 4 * a + r4] = c16.astype(BF16)


def _inproj_kernel(x_ref, g_ref, wz_ref, wxbc_ref, wdt_ref, wq_ref, wk_ref, wv_ref, ones_ref,
                   qg_ref, kg_ref, rope_ref,
                   xz_out, dt_out, q1_out, q4_out, q16_out, kv1_out, kv4_out, kv16_out,
                   slab_ref, slab2_ref):
    xn = _rms(x_ref[...], g_ref[...]).astype(BF16)

    cos = jnp.concatenate([rope_ref[:, 0:LANES]] * 4, axis=1)
    sdn = jnp.concatenate([rope_ref[:, LANES:2 * LANES]] * 4, axis=1)
    sup = jnp.concatenate([rope_ref[:, 2 * LANES:]] * 4, axis=1)

    def norm_rope(t, gain_ref, scale):
        ssq = _dot((t * t).astype(BF16), ones_ref[...])
        t = t * lax.rsqrt(ssq * (1.0 / HEAD_DIM) + EPS) * gain_ref[...]
        half = ROPE_DIMS // 2
        t = (t * cos + pltpu.roll(t, half, 1) * sdn
             + pltpu.roll(t, ATT_WIDTH - half, 1) * sup)
        return t * scale

    tq = _dot(xn, wq_ref[...])
    xz_out[:, CONV_CH:] = _dot(xn, wz_ref[...]).astype(BF16)
    _emit_layouts(norm_rope(tq, qg_ref, LOG2E / math.sqrt(HEAD_DIM)), q1_out, q4_out, q16_out,
                  slab_ref.at[0], slab2_ref.at[0])
    tk = _dot(xn, wk_ref[...])
    xz_out[:, :CONV_CH] = _dot(xn, wxbc_ref[...]).astype(BF16)
    _emit_layouts(norm_rope(tk, kg_ref, 1.0), kv1_out.at[0], kv4_out.at[0], kv16_out.at[0],
                  slab_ref.at[1], slab2_ref.at[1])
    tv = _dot(xn, wv_ref[...])
    dt_out[...] = _dot(xn, wdt_ref[...])
    _emit_layouts(tv, kv1_out.at[1], kv4_out.at[1], kv16_out.at[1], slab_ref.at[2], slab2_ref.at[2])


def _inproj(x2, bsz, seq, prm, tm):
    t_tokens = x2.shape[0]
    tiles_per_seq = seq // tm
    tok = lambda w: pl.BlockSpec((tm, w), lambda i: (i, 0))
    rope = pl.BlockSpec((tm, 3 * LANES), lambda i: (i % tiles_per_seq, 0))
    npair = ATT_WIDTH // LANES
    cls = lambda d: pl.BlockSpec((1, npair, d, tm // d, LANES),
                                 lambda i: (i // tiles_per_seq, 0, 0, i % tiles_per_seq, 0))
    cls2 = lambda d: pl.BlockSpec((2, 1, npair, d, tm // d, LANES),
                                  lambda i: (0, i // tiles_per_seq, 0, 0, i % tiles_per_seq, 0))
    cls_shape = lambda d: (bsz, npair, d, seq // d, LANES)
    out_shape = (
        jax.ShapeDtypeStruct((t_tokens, CONV_CH + SSD_WIDTH), BF16),
        jax.ShapeDtypeStruct((t_tokens, LANES), F32),
        jax.ShapeDtypeStruct((npair, t_tokens, LANES), BF16),
        jax.ShapeDtypeStruct(cls_shape(4), BF16),
        jax.ShapeDtypeStruct(cls_shape(16), BF16),
        jax.ShapeDtypeStruct((2, npair, t_tokens, LANES), BF16),
        jax.ShapeDtypeStruct((2,) + cls_shape(4), BF16),
        jax.ShapeDtypeStruct((2,) + cls_shape(16), BF16),
    )
    out_specs = (tok(CONV_CH + SSD_WIDTH), tok(LANES),
                 pl.BlockSpec((npair, tm, LANES), lambda i: (0, i, 0)), cls(4), cls(16),
                 pl.BlockSpec((2, npair, tm, LANES), lambda i: (0, 0, i, 0)), cls2(4), cls2(16))
    return pl.pallas_call(
        _inproj_kernel,
        grid=(t_tokens // tm,),
        in_specs=[
            tok(D_MODEL), _const_spec((1, D_MODEL)),
            _const_spec((D_MODEL, SSD_WIDTH)), _const_spec((D_MODEL, CONV_CH)),
            _const_spec((D_MODEL, LANES)), _const_spec((D_MODEL, ATT_WIDTH)),
            _const_spec((D_MODEL, ATT_WIDTH)), _const_spec((D_MODEL, ATT_WIDTH)),
            _const_spec((ATT_WIDTH, ATT_WIDTH)),
            _const_spec((1, ATT_WIDTH)), _const_spec((1, ATT_WIDTH)),
            rope,
        ],
        out_specs=out_specs,
        out_shape=out_shape,
        scratch_shapes=[pltpu.VMEM((3, ATT_WIDTH // LANES, tm, LANES), F32),
                        pltpu.VMEM((3, ATT_WIDTH // LANES, tm, LANES), F32)],
        compiler_params=_params(("parallel",)),
        name="inproj",
    )(x2, prm["mix_g"], prm["w_z"], prm["w_xbc"], prm["w_dt"], prm["w_q"], prm["w_k"], prm["w_v"],
      prm["head_ones"], prm["qg"], prm["kg"], prm["rope"][seq])


def _split3(a):
    a1 = a.astype(BF16)
    r1 = a - a1.astype(F32)
    a2 = r1.astype(BF16)
    a3 = (r1 - a2.astype(F32)).astype(BF16)
    return a1, a2, a3


def _dot_split(lhs_bf16, a):
    a1, a2, a3 = _split3(a)
    return _dot(lhs_bf16, a1) + _dot(lhs_bf16, a2) + _dot(lhs_bf16, a3)


def _ssd_kernel(nsteps, xbc_ref, xprev_ref, xnext_ref, z_ref, dt_ref, convw_ref, convb_ref, alog_ref,
                dtbias_ref, drow_ref, ng_ref, out_ref, ext_ref, carry_ref, prevb_ref, conv_ref):
    q = SSD_CHUNK
    nh = SSD_HEADS
    width = SSD_WIDTH // SSD_GROUPS
    pad = (SSD_CONV - 1) // 2
    s = pl.program_id(1)
    backward = s < nsteps
    step = jnp.where(backward, nsteps - 1 - s, s - nsteps)

    lane = _lane((q, LANES))
    row = lax.broadcasted_iota(jnp.int32, (q, LANES), 0)
    erow = lax.broadcasted_iota(jnp.int32, (LANES, 2 * SSD_WIDTH), 0)
    ecol = lax.broadcasted_iota(jnp.int32, (LANES, 2 * SSD_WIDTH), 1)
    expand = (erow == ecol // HEAD_DIM).astype(BF16)
    ltri = (lane <= row).astype(BF16)
    utri = (lane >= row).astype(BF16)

    lane_c = _lane((q, SSD_CPS * LANES)) % LANES

    def side_by_side(t):
        return jnp.concatenate([t[j * q:(j + 1) * q] for j in range(SSD_CPS)], axis=1)

    def stacked(t):
        return jnp.concatenate([t[:, j * LANES:(j + 1) * LANES] for j in range(SSD_CPS)], axis=0)

    def expand_rows(v, lo, hi):
        return _dot(v.astype(BF16), expand[:, lo:hi])

    def expand_rows_exact(v, lo, hi):
        v1, v2, v3 = _split3(v)
        e = expand[:, lo:hi]
        return _dot(v1, e) + _dot(v2, e) + _dot(v3, e)

    def step_decays():
        raw = dt_ref[0] + dtbias_ref[...]
        softplus = jnp.maximum(raw, 0.0) + jnp.log(1.0 + jnp.exp(-jnp.abs(raw)))
        dt = jnp.where(_lane(raw.shape) < 2 * nh, softplus, 0.0)
        return side_by_side(dt), side_by_side(dt * (-jnp.exp(alog_ref[...])))

    def chunk_states(xs, bm16, w_full):
        xw = (xs * w_full).astype(BF16)
        parts = []
        for g in range(SSD_GROUPS):
            bt = bm16[:, g * SSD_STATE:(g + 1) * SSD_STATE].astype(F32).T.astype(BF16)
            parts.append(_dot(bt, xw[:, g * width:(g + 1) * width]))
        return jnp.concatenate(parts, axis=1)

    @pl.when(s == 0)
    def _():
        carry_ref[...] = jnp.zeros_like(carry_ref)

    @pl.when(s == nsteps)
    def _():
        carry_ref[...] = jnp.zeros_like(carry_ref)

    @pl.when(backward)
    def _():
        rows = SSD_CPS * q
        prev = xprev_ref[0]
        nxt = xnext_ref[0]
        ext_ref[0:BF16_ROWS, :] = jnp.where(step > 0, prev, jnp.zeros_like(prev))
        ext_ref[BF16_ROWS:BF16_ROWS + rows, :] = xbc_ref[0]
        ext_ref[BF16_ROWS + rows:, :] = jnp.where(step < nsteps - 1, nxt, jnp.zeros_like(nxt))

        win_rows = q + 2 * BF16_ROWS
        srow = lax.broadcasted_iota(jnp.int32, (q, win_rows), 0)
        scol = lax.broadcasted_iota(jnp.int32, (q, win_rows), 1)
        shifts = {k: (scol == srow + BF16_ROWS + k - pad).astype(BF16)
                  for k in range(SSD_CONV) if k != pad}

        bmask = (lane_c >= nh) & (lane_c < 2 * nh)
        dt, a = step_decays()
        sb = _dot_split(utri, a)
        wb = jnp.where(bmask, jnp.exp(sb[0:1, :] - sb) * dt, 0.0)
        w_full = expand_rows(stacked(wb), SSD_WIDTH, 2 * SSD_WIDTH)
        decay = expand_rows_exact(stacked(jnp.where(bmask[0:1], jnp.exp(sb[0:1, :]), 0.0)),
                                  SSD_WIDTH, 2 * SSD_WIDTH)

        def conv_dots(j):
            win = ext_ref[j * q:j * q + win_rows, :]
            return win, {k: _dot(sh, win) for k, sh in shifts.items()}

        def conv_finish(j, win, dots):
            acc = convb_ref[...] + win[BF16_ROWS:BF16_ROWS + q].astype(F32) * convw_ref[pad:pad + 1, :]
            for k, d in dots.items():
                acc = acc + d * convw_ref[k:k + 1, :]
            conv_ref[step * SSD_CPS + j] = _silu(acc).astype(BF16)

        pending = conv_dots(0)
        for j in range(SSD_CPS):
            ahead = conv_dots(j + 1) if j + 1 < SSD_CPS else None
            conv_finish(j, *pending)
            pending = ahead

        states = []
        for j in range(SSD_CPS):
            xb16 = conv_ref[step * SSD_CPS + j]
            states.append(chunk_states(xb16[:, :SSD_WIDTH].astype(F32),
                                       xb16[:, SSD_WIDTH:SSD_WIDTH + SSD_GROUPS * SSD_STATE],
                                       w_full[j * q:(j + 1) * q]))
        carry = carry_ref[...]
        for j in reversed(range(SSD_CPS)):
            prevb_ref[step * SSD_CPS + j] = carry.astype(BF16)
            carry = carry * decay[j:j + 1] + states[j]
        carry_ref[...] = carry

    @pl.when(jnp.logical_not(backward))
    def _():
        ii = lax.broadcasted_iota(jnp.int32, (q, q), 0)
        jj = lax.broadcasted_iota(jnp.int32, (q, q), 1)
        lane_p = _lane((q, LANES))
        fmask = lane_c < nh
        both = lane_c < 2 * nh

        dt, a = step_decays()
        sums = _dot_split(jnp.concatenate([ltri, utri], axis=0), a)
        cf, sb = sums[:q], sums[q:]
        ldt = jnp.where(both, jnp.log2(dt), 0.0)
        pack_all = jnp.where(fmask, cf * LOG2E,
                             jnp.where(both, sb * LOG2E, pltpu.roll(ldt, 2 * nh, 1)))
        pack_t_all = pack_all.T
        dsum = dt + pltpu.roll(dt, SSD_CPS * LANES - nh, 1)
        dsum_t_all = jnp.where(fmask, jnp.log2(dsum), 0.0).T
        edge_all = expand_rows(stacked(jnp.where(both, jnp.exp2(pack_all), 0.0)), 0, 2 * SSD_WIDTH)
        last = cf[q - 1:q, :]
        w_full = expand_rows(stacked(jnp.where(fmask, jnp.exp(last - cf) * dt, 0.0)), 0, SSD_WIDTH)
        decay = expand_rows_exact(stacked(jnp.where(fmask[0:1], jnp.exp(last), 0.0)), 0, SSD_WIDTH)

        states = []
        for j in range(SSD_CPS):
            xb16 = conv_ref[step * SSD_CPS + j]
            states.append(chunk_states(xb16[:, :SSD_WIDTH].astype(F32),
                                       xb16[:, SSD_WIDTH:SSD_WIDTH + SSD_GROUPS * SSD_STATE],
                                       w_full[j * q:(j + 1) * q]))
        prevs = []
        carry = carry_ref[...]
        for j in range(SSD_CPS):
            prevs.append(carry.astype(BF16))
            carry = carry * decay[j:j + 1] + states[j]
        carry_ref[...] = carry

        for j in range(SSD_CPS):
            c = step * SSD_CPS + j
            pack = pack_all[:, j * LANES:(j + 1) * LANES]
            pack_t = pack_t_all[j * LANES:(j + 1) * LANES]
            dsum_t = dsum_t_all[j * LANES:(j + 1) * LANES]
            edge = edge_all[j * q:(j + 1) * q]
            xb16 = conv_ref[c]
            xs16 = xb16[:, :SSD_WIDTH]
            xs = xs16.astype(F32)
            bm16 = xb16[:, SSD_WIDTH:SSD_WIDTH + SSD_GROUPS * SSD_STATE]
            cm16 = xb16[:, SSD_WIDTH + SSD_GROUPS * SSD_STATE:]

            cb = [_dot_nt(cm16[:, g * SSD_STATE:(g + 1) * SSD_STATE],
                          bm16[:, g * SSD_STATE:(g + 1) * SSD_STATE]) for g in range(SSD_GROUPS)]
            mats = []
            for h in range(nh):
                low_row = pack_t[h:h + 1, :] - pack_t[2 * nh + h:2 * nh + h + 1, :]
                up_row = pack_t[nh + h:nh + h + 1, :] - pack_t[3 * nh + h:3 * nh + h + 1, :]
                expo = jnp.where(jj < ii, pack[:, h:h + 1] - low_row,
                                 jnp.where(jj > ii, pack[:, nh + h:nh + h + 1] - up_row,
                                           dsum_t[h:h + 1, :]))
                mats.append((cb[h // (nh // SSD_GROUPS)] * jnp.exp2(expo)).astype(BF16))

            y_parts = []
            for p in range(nh // 2):
                xp = xs16[:, p * LANES:(p + 1) * LANES]
                zero = jnp.zeros_like(xp)
                rhs = jnp.concatenate([jnp.where(lane_p < HEAD_DIM, xp, zero),
                                       jnp.where(lane_p >= HEAD_DIM, xp, zero)], axis=0)
                lhs = jnp.concatenate([mats[2 * p], mats[2 * p + 1]], axis=1)
                y_parts.append(_dot(lhs, rhs))
            y = jnp.concatenate(y_parts, axis=1)

            prev_f = prevs[j]
            prev_b = prevb_ref[c]
            off_parts = []
            for g in range(SSD_GROUPS):
                st = jnp.concatenate([prev_f[:, g * width:(g + 1) * width],
                                      prev_b[:, g * width:(g + 1) * width]], axis=1)
                res = _dot(cm16[:, g * SSD_STATE:(g + 1) * SSD_STATE], st)
                off_parts.append(res[:, :width] * edge[:, g * width:(g + 1) * width]
                                 + res[:, width:] * edge[:, SSD_WIDTH + g * width:
                                                         SSD_WIDTH + (g + 1) * width])
            y = y + jnp.concatenate(off_parts, axis=1) + drow_ref[...] * xs

            y = y * _silu(z_ref[0, j * q:(j + 1) * q, :].astype(F32))
            out_ref[0, j * q:(j + 1) * q, :] = _rms(y, ng_ref[...]).astype(BF16)


def _ssd(xz, dt, prm, bsz, seq):
    q = SSD_CHUNK
    rows = SSD_CPS * q
    assert seq % rows == 0
    nsteps = seq // rows
    nc = seq // q
    halo_blocks = rows // BF16_ROWS
    conv_step = lambda s: jnp.where(s < nsteps, nsteps - 1 - s, 0)
    both_step = lambda s: jnp.where(s < nsteps, nsteps - 1 - s, s - nsteps)
    fwd_step = lambda s: jnp.maximum(s - nsteps, 0)
    xz3 = xz.reshape(bsz, seq, CONV_CH + SSD_WIDTH)
    gate_blk = CONV_CH // SSD_WIDTH
    dt3 = dt.reshape(bsz, seq, LANES)
    in_specs = [
        pl.BlockSpec((1, rows, CONV_CH), lambda b, s: (b, conv_step(s), 0)),
        pl.BlockSpec((1, BF16_ROWS, CONV_CH),
                     lambda b, s: (b, jnp.maximum(conv_step(s) * halo_blocks - 1, 0), 0)),
        pl.BlockSpec((1, BF16_ROWS, CONV_CH),
                     lambda b, s: (b, jnp.minimum((conv_step(s) + 1) * halo_blocks,
                                                  seq // BF16_ROWS - 1), 0)),
        pl.BlockSpec((1, rows, SSD_WIDTH), lambda b, s: (b, fwd_step(s), gate_blk)),
        pl.BlockSpec((1, rows, LANES), lambda b, s: (b, both_step(s), 0)),
        _const_spec((SSD_CONV, CONV_CH)), _const_spec((1, CONV_CH)),
        _const_spec((1, LANES)), _const_spec((1, LANES)),
        _const_spec((1, SSD_WIDTH)), _const_spec((1, SSD_WIDTH)),
    ]
    out_spec = pl.BlockSpec((1, rows, SSD_WIDTH), lambda b, s: (b, fwd_step(s), 0))
    out = pl.pallas_call(
        functools.partial(_ssd_kernel, nsteps),
        grid=(bsz, 2 * nsteps),
        in_specs=in_specs,
        out_specs=out_spec,
        out_shape=jax.ShapeDtypeStruct((bsz, seq, SSD_WIDTH), BF16),
        scratch_shapes=[
            pltpu.VMEM((rows + 2 * BF16_ROWS, CONV_CH), BF16),
            pltpu.VMEM((SSD_STATE, SSD_WIDTH), F32),
            pltpu.VMEM((nc, SSD_STATE, SSD_WIDTH), BF16),
            pltpu.VMEM((nc, q, CONV_CH), BF16),
        ],
        compiler_params=_params(("parallel", "arbitrary")),
        name="ssd",
    )(xz3, xz3, xz3, xz3, dt3, prm["conv_w"], prm["conv_b"], prm["alog_row"], prm["dtbias_row"],
      prm["d_row"], prm["ssd_ng"])
    return out.reshape(bsz * seq, SSD_WIDTH)


def _attn_unit(q_tile, kwin_ref, vwin_ref, row0, bias, m_old):
    low = _lane((ATT_Q, LANES)) < HEAD_DIM
    zero = jnp.zeros_like(q_tile)
    lhs = jnp.concatenate([jnp.where(low, q_tile, zero), jnp.where(low, zero, q_tile)], axis=0)
    sc = _dot_nt(lhs, kwin_ref[pl.ds(row0, ATT_KEYS), :]) + bias
    part = jnp.maximum(sc[:, :LANES], sc[:, LANES:])
    if m_old is not None:
        ninf = jnp.full_like(m_old, NEG_INF)
        part = jnp.maximum(part, jnp.concatenate([jnp.where(low, m_old, ninf),
                                                  jnp.where(low, ninf, m_old)], axis=0))
    mx = jnp.max(part, axis=-1, keepdims=True)
    pr = jnp.exp2(sc - mx).astype(BF16)
    ones = jnp.ones((ATT_KEYS, LANES), BF16)
    vcat = jnp.concatenate([vwin_ref[pl.ds(row0, ATT_KEYS), :], ones], axis=1)
    res = _dot(pr, vcat)
    num = jnp.where(low, res[:ATT_Q, :LANES], res[ATT_Q:, :LANES])
    den = jnp.where(low, res[:ATT_Q, LANES:], res[ATT_Q:, LANES:])
    mxp = jnp.where(low, jnp.broadcast_to(mx[:ATT_Q], (ATT_Q, LANES)),
                    jnp.broadcast_to(mx[ATT_Q:], (ATT_Q, LANES)))
    return num, den, mxp


def _attn_kernel(nblk, q1, kv1c, kv1p, kv1n, q4, kv4c, kv4p, kv4n, q16, kv16c, kv16p, kv16n,
                 o_out, kwin, vwin, acc_ref, den_ref, max_ref, bias_ref):
    q1, q4, q16, o_out = (r.at[0] for r in (q1, q4, q16, o_out))
    k1c, k1p, k1n, v1c, v1p, v1n = (r.at[i].at[0] for i in range(2) for r in (kv1c, kv1p, kv1n))
    k4c, k4p, k4n, v4c, v4p, v4n = (r.at[i].at[0] for i in range(2) for r in (kv4c, kv4p, kv4n))
    k16c, k16p, k16n, v16c, v16p, v16n = (r.at[i].at[0] for i in range(2)
                                          for r in (kv16c, kv16p, kv16n))
    n = pl.program_id(2)
    half = BAND_HALF
    rows2 = 2 * ATT_Q

    @pl.when((pl.program_id(0) == 0) & (pl.program_id(1) == 0) & (n == 0))
    def _():
        qi = lax.broadcasted_iota(jnp.int32, (rows2, ATT_KEYS), 0) % ATT_Q
        kt = lax.broadcasted_iota(jnp.int32, (rows2, ATT_KEYS), 1)
        band = jnp.where(jnp.abs(kt - half - qi) <= half, 0.0, NEG_INF)
        bias_ref[0] = band
        bias_ref[1] = jnp.where(kt < half, NEG_INF, band)
        bias_ref[2] = jnp.where(kt >= half + ATT_Q, NEG_INF, band)

    def bias_for(first_sub, last_sub):
        idx = jnp.where((n == 0) & first_sub, 1, jnp.where((n == nblk - 1) & last_sub, 2, 0))
        return bias_ref[idx]

    def fill(win, base, prev, cur, nxt, rows):
        win[pl.ds(base, half), :] = prev
        win[pl.ds(base + half, rows), :] = cur
        win[pl.ds(base + half + rows, half), :] = nxt

    def load_state(rows):
        return acc_ref[rows, :], den_ref[rows, :], max_ref[rows, :]

    def merge(old, num, den, mxp):
        scale = jnp.exp2(old[2] - mxp)
        return old[0] * scale + num, old[1] * scale + den, mxp

    def store_state(rows, new):
        acc_ref[rows, :] = new[0]
        den_ref[rows, :] = new[1]
        max_ref[rows, :] = new[2]

    span16 = ATT_KEYS

    def fill16(r, carry):
        base = pl.multiple_of(r * span16, span16)
        fill(kwin, base, k16p[0, r], k16c[0, r], k16n[0, r], ATT_Q)
        fill(vwin, base, v16p[0, r], v16c[0, r], v16n[0, r], ATT_Q)
        return carry

    lax.fori_loop(0, 16, fill16, 0)

    def group16(g, carry):
        new = []
        for u in range(ATT_GROUP):
            r = g * ATT_GROUP + u
            new.append(_attn_unit(q16[0, r], kwin, vwin, pl.multiple_of(r * span16, span16),
                                  bias_for(True, True), None))
        for u in range(ATT_GROUP):
            store_state(pl.ds(g * ATT_GROUP + u, ATT_Q, stride=16), new[u])
        return carry

    lax.fori_loop(0, 16 // ATT_GROUP, group16, 0)

    rows4 = ATT_BLK // 4
    nsub4 = rows4 // ATT_Q
    span4 = rows4 + 2 * half
    for r in range(4):
        fill(kwin, r * span4, k4p[0, r], k4c[0, r], k4n[0, r], rows4)
        fill(vwin, r * span4, v4p[0, r], v4c[0, r], v4n[0, r], rows4)

    ncls = ATT_GROUP // nsub4

    def group4(g, carry):
        units = [(g * ncls + u, j) for u in range(ncls) for j in range(nsub4)]
        rows = [pl.ds(r + 4 * ATT_Q * j, ATT_Q, stride=4) for r, j in units]
        old = [load_state(rw) for rw in rows]
        new = []
        for (r, j), prev in zip(units, old):
            unit = _attn_unit(q4[0, r, j * ATT_Q:(j + 1) * ATT_Q, :], kwin, vwin,
                              pl.multiple_of(r * span4, ATT_Q) + j * ATT_Q,
                              bias_for(j == 0, j == nsub4 - 1), prev[2])
            new.append(merge(prev, *unit))
        for rw, nw in zip(rows, new):
            store_state(rw, nw)
        return carry

    lax.fori_loop(0, 4 // ncls, group4, 0)

    fill(kwin, 0, k1p[0], k1c[0], k1n[0], ATT_BLK)
    fill(vwin, 0, v1p[0], v1c[0], v1n[0], ATT_BLK)
    nsub1 = ATT_BLK // ATT_Q

    def group1(g, carry):
        for u in range(ATT_GROUP):
            j = g * ATT_GROUP + u
            r0 = pl.multiple_of(j * ATT_Q, ATT_Q)
            rows = pl.ds(r0, ATT_Q)
            old = load_state(rows)
            unit = _attn_unit(q1[0, rows, :], kwin, vwin, r0, bias_for(j == 0, j == nsub1 - 1),
                              old[2])
            acc_new, den_new, _ = merge(old, *unit)
            o_out[0, rows, :] = (acc_new / den_new).astype(BF16)
        return carry

    lax.fori_loop(0, nsub1 // ATT_GROUP, group1, 0)


def _dilated_attention(qkv, bsz, seq):
    (q1, q4, q16), (kv1, kv4, kv16) = qkv
    assert tuple(d for _, d in DILATED_PATTERNS) == (1, 4, 16)
    assert all(w // (2 * d) == BAND_HALF for w, d in DILATED_PATTERNS)
    assert seq % ATT_BLK == 0 and seq // 16 >= 2 * ATT_Q
    nblk = seq // ATT_BLK
    npair = ATT_WIDTH // LANES
    def specs(dil):
        rows = ATT_BLK // dil
        per_blk = rows // BAND_HALF
        last = seq // dil // BAND_HALF - 1
        before = lambda n: jnp.maximum(n * per_blk - 1, 0)
        after = lambda n: jnp.minimum((n + 1) * per_blk, last)
        if dil == 1:
            blk = lambda r: (1, 1, r, LANES)
            idx = lambda b, p, m: (p, b, m, 0)
        else:
            blk = lambda r: (1, 1, dil, r, LANES)
            idx = lambda b, p, m: (b, p, 0, m, 0)
        return [
            pl.BlockSpec(blk(rows), lambda b, p, n: idx(b, p, n)),
            pl.BlockSpec((2,) + blk(rows), lambda b, p, n: (0,) + idx(b, p, n)),
            pl.BlockSpec((2,) + blk(BAND_HALF), lambda b, p, n: (0,) + idx(b, p, before(n))),
            pl.BlockSpec((2,) + blk(BAND_HALF), lambda b, p, n: (0,) + idx(b, p, after(n))),
        ]

    q1 = q1.reshape(npair, bsz, seq, LANES)
    kv1 = kv1.reshape(2, npair, bsz, seq, LANES)
    args = [q1] + [kv1] * 3 + [q4] + [kv4] * 3 + [q16] + [kv16] * 3
    win_rows = 16 * ATT_KEYS
    out = pl.pallas_call(
        functools.partial(_attn_kernel, nblk),
        grid=(bsz, npair, nblk),
        in_specs=specs(1) + specs(4) + specs(16),
        out_specs=pl.BlockSpec((1, 1, ATT_BLK, LANES), lambda b, p, n: (b, p, n, 0)),
        out_shape=jax.ShapeDtypeStruct((bsz, npair, seq, LANES), BF16),
        scratch_shapes=[pltpu.VMEM((win_rows, LANES), BF16),
                        pltpu.VMEM((win_rows, LANES), BF16),
                        pltpu.VMEM((ATT_BLK, LANES), F32),
                        pltpu.VMEM((ATT_BLK, LANES), F32),
                        pltpu.VMEM((ATT_BLK, LANES), F32),
                        pltpu.VMEM((3, 2 * ATT_Q, ATT_KEYS), F32)],
        compiler_params=_params(("arbitrary", "arbitrary", "arbitrary")),
        name="attn",
    )(*args)
    return out


def _memkv_kernel(mem_ref, g_ref, wkv_ref, kg_ref, kt_out, v_out):
    mn = _rms(mem_ref[0], g_ref[...]).astype(BF16)
    kv = _dot(mn, wkv_ref[...])
    scale = 1.0 / math.sqrt(XATT_HEAD_DIM)
    for h in range(XATT_HEADS):
        kh = kv[:, h * XATT_HEAD_DIM:(h + 1) * XATT_HEAD_DIM]
        kh = _rms(kh, kg_ref[...]) * scale
        kt_out[0, h * XATT_HEAD_DIM:(h + 1) * XATT_HEAD_DIM, :] = kh.T.astype(BF16)
    v_out[0] = kv[:, D_MODEL:].astype(BF16)


def _memkv(mem, prm):
    bsz = mem.shape[0]
    return pl.pallas_call(
        _memkv_kernel,
        grid=(bsz,),
        in_specs=[pl.BlockSpec((1, MEM_LEN, D_MODEL), lambda b: (b, 0, 0)),
                  _const_spec((1, D_MODEL)), _const_spec((D_MODEL, 2 * D_MODEL)),
                  _const_spec((1, XATT_HEAD_DIM))],
        out_specs=(pl.BlockSpec((1, D_MODEL, MEM_LEN), lambda b: (b, 0, 0)),
                   pl.BlockSpec((1, MEM_LEN, D_MODEL), lambda b: (b, 0, 0))),
        out_shape=(jax.ShapeDtypeStruct((bsz, D_MODEL, MEM_LEN), BF16),
                   jax.ShapeDtypeStruct((bsz, MEM_LEN, D_MODEL), BF16)),
        compiler_params=_params(("parallel",)),
        name="memkv",
    )(mem, prm["mem_g"], prm["w_kv"], prm["xkg"])


def _post_kernel(x_ref, ssd_ref, att_ref, og_ref, wo1_ref, wo2_ref, g_ref, wq_ref, qg_ref, kt_ref,
                 v_ref, wo_ref, h_out):
    tm = x_ref.shape[0]
    halves = [slice(i * tm // 2, (i + 1) * tm // 2) for i in range(2)]

    def mix(rows):
        att = jnp.concatenate([att_ref[0, p, rows, :] for p in range(ATT_WIDTH // LANES)], axis=1)
        att = _rms(att.astype(F32), og_ref[...]).astype(BF16)
        return x_ref[rows, :] + _dot(ssd_ref[rows, :], wo1_ref[...]) + _dot(att, wo2_ref[...])

    def query(h):
        return _dot(_rms(h, g_ref[...]).astype(BF16), wq_ref[...])

    def head(hd):
        return slice(hd * XATT_HEAD_DIM, (hd + 1) * XATT_HEAD_DIM)

    def scores(qf, hd):
        return _dot(_rms(qf[:, head(hd)], qg_ref[...]).astype(BF16), kt_ref[0, head(hd), :])

    def weighted(sc, hd):
        mx = jnp.max(sc, axis=-1, keepdims=True)
        pr = jnp.exp(sc - mx)
        den = jnp.sum(pr, axis=-1, keepdims=True)
        return _dot(pr.astype(BF16), v_ref[0, :, head(hd)]) / den

    hs = [mix(rows) for rows in halves]
    qs = [query(h) for h in hs]
    parts = [[], []]
    for hd in range(XATT_HEADS):
        scs = [scores(qf, hd) for qf in qs]
        for i, sc in enumerate(scs):
            parts[i].append(weighted(sc, hd))
    os_ = [jnp.concatenate(p, axis=1).astype(BF16) for p in parts]
    for rows, h, o in zip(halves, hs, os_):
        h_out[rows, :] = h + _dot(o, wo_ref[...])


def _post(x2, ssd, att, kt, vx, prm, seq, tm):
    t_tokens = x2.shape[0]
    tiles_per_seq = seq // tm
    tok = lambda w: pl.BlockSpec((tm, w), lambda i: (i, 0))
    half = D_MODEL // 2
    return pl.pallas_call(
        _post_kernel,
        grid=(t_tokens // tm,),
        in_specs=[tok(D_MODEL), tok(SSD_WIDTH),
                  pl.BlockSpec((1, ATT_WIDTH // LANES, tm, LANES),
                               lambda i: (i // tiles_per_seq, 0, i % tiles_per_seq, 0)),
                  _const_spec((1, ATT_WIDTH)),
                  _const_spec((half, D_MODEL)), _const_spec((half, D_MODEL)),
                  _const_spec((1, D_MODEL)), _const_spec((D_MODEL, D_MODEL)),
                  _const_spec((1, XATT_HEAD_DIM)),
                  pl.BlockSpec((1, D_MODEL, MEM_LEN), lambda i: (i // tiles_per_seq, 0, 0)),
                  pl.BlockSpec((1, MEM_LEN, D_MODEL), lambda i: (i // tiles_per_seq, 0, 0)),
                  _const_spec((D_MODEL, D_MODEL))],
        out_specs=tok(D_MODEL),
        out_shape=jax.ShapeDtypeStruct((t_tokens, D_MODEL), F32),
        compiler_params=_params(("parallel",)),
        name="post",
    )(x2, ssd, att, prm["att_og"], prm["w_out1"], prm["w_out2"], prm["xatt_g"], prm["w_xq"],
      prm["xqg"], kt, vx, prm["w_xo"])


def _mlp_kernel(ff_chunk, h_ref, g_ref, w1_ref, w2_ref, y_out):
    h = h_ref[...]
    hm = _rms(h, g_ref[...]).astype(BF16)
    acc = h
    for c in range(D_FF // ff_chunk):
        sl = slice(c * ff_chunk, (c + 1) * ff_chunk)
        a = jnp.maximum(_dot(hm, w1_ref[:, sl]), 0.0)
        acc = acc + _dot((a * a).astype(BF16), w2_ref[sl, :])
    y_out[...] = acc


def _mlp(h, prm, tm):
    t_tokens = h.shape[0]
    tok = pl.BlockSpec((tm, D_MODEL), lambda i: (i, 0))
    return pl.pallas_call(
        functools.partial(_mlp_kernel, 1024),
        grid=(t_tokens // tm,),
        in_specs=[tok, _const_spec((1, D_MODEL)), _const_spec((D_MODEL, D_FF)),
                  _const_spec((D_FF, D_MODEL))],
        out_specs=tok,
        out_shape=jax.ShapeDtypeStruct((t_tokens, D_MODEL), F32),
        compiler_params=_params(("parallel",)),
        name="mlp",
    )(h, prm["mlp_g"], prm["w1"], prm["w2"])


def _rope_tables(seq):
    half = ROPE_DIMS // 2
    dim = jnp.arange(LANES) % HEAD_DIM
    inv_freq = jnp.power(jnp.float32(ROPE_THETA), -(dim % half).astype(F32) / half)
    ang = jnp.arange(seq).astype(F32)[:, None] * inv_freq[None, :]
    sin = jnp.sin(ang)
    cos = jnp.where(dim < ROPE_DIMS, jnp.cos(ang), 1.0)
    sdn = jnp.where((dim >= half) & (dim < ROPE_DIMS), sin, 0.0)
    sup = jnp.where(dim < half, -sin, 0.0)
    return jnp.concatenate([cos, sdn, sup], axis=1)


def _prepare(seqs, mix_norm_g, w_in, conv_w, conv_b, ssd_A_log, ssd_dt_bias, ssd_D, ssd_norm_g,
             att_q_norm_g, att_k_norm_g, att_out_norm_g, w_out, xatt_norm_g, mem_norm_g, xatt_wq,
             xatt_wkv, xatt_q_norm_g, xatt_k_norm_g, xatt_wo, mlp_norm_g, mlp_w1, mlp_w2):
    row = lambda t: t.reshape(1, -1).astype(F32)
    pad_lanes = lambda t: jnp.pad(t, ((0, 0), (0, LANES - t.shape[1])))
    head_id = jnp.arange(ATT_WIDTH) // HEAD_DIM
    prm = {
        "mix_g": row(mix_norm_g),
        "w_z": w_in[:, OFF_Z:OFF_XBC].astype(BF16),
        "w_xbc": w_in[:, OFF_XBC:OFF_DT].astype(BF16),
        "w_dt": pad_lanes(w_in[:, OFF_DT:OFF_Q]).astype(BF16),
        "w_q": w_in[:, OFF_Q:OFF_K].astype(BF16),
        "w_k": w_in[:, OFF_K:OFF_V].astype(BF16),
        "w_v": w_in[:, OFF_V:IN_COLS].astype(BF16),
        "head_ones": (head_id[:, None] == head_id[None, :]).astype(BF16),
        "qg": jnp.tile(row(att_q_norm_g), (1, ATT_HEADS)),
        "kg": jnp.tile(row(att_k_norm_g), (1, ATT_HEADS)),
        "conv_w": conv_w.astype(F32),
        "conv_b": row(conv_b),
        "alog_row": pad_lanes(row(ssd_A_log)),
        "dtbias_row": pad_lanes(row(ssd_dt_bias)),
        "d_row": jnp.repeat(row(ssd_D), HEAD_DIM, axis=1),
        "ssd_ng": row(ssd_norm_g),
        "att_og": row(att_out_norm_g),
        "w_out1": w_out[:SSD_WIDTH].astype(BF16),
        "w_out2": w_out[SSD_WIDTH:].astype(BF16),
        "xatt_g": row(xatt_norm_g),
        "mem_g": row(mem_norm_g),
        "w_xq": xatt_wq.astype(BF16),
        "w_kv": xatt_wkv.astype(BF16),
        "xqg": row(xatt_q_norm_g),
        "xkg": row(xatt_k_norm_g),
        "w_xo": xatt_wo.astype(BF16),
        "mlp_g": row(mlp_norm_g),
        "w1": mlp_w1.astype(BF16),
        "w2": mlp_w2.astype(BF16),
        "rope": {seq: _rope_tables(seq) for seq in set(seqs)},
    }
    return prm


IN_TILE = 512
OUT_TILE = 1024


def _layer(x, mem, prm):
    bsz, seq, _ = x.shape
    x2 = x.reshape(bsz * seq, D_MODEL)
    outs = _inproj(x2, bsz, seq, prm, min(IN_TILE, seq))
    ssd = _ssd(outs[0], outs[1], prm, bsz, seq)
    att = _dilated_attention((outs[2:5], outs[5:8]), bsz, seq)
    kt, vx = _memkv(mem, prm)
    h = _post(x2, ssd, att, kt, vx, prm, seq, min(OUT_TILE, seq))
    y = _mlp(h, prm, min(OUT_TILE, seq))
    return y.reshape(bsz, seq, D_MODEL)


def kernel(x_prompt, x_sample, mem_prompt, mem_sample, mix_norm_g, w_in, conv_w, conv_b, ssd_A_log,
           ssd_dt_bias, ssd_D, ssd_norm_g, att_q_norm_g, att_k_norm_g, att_out_norm_g, w_out,
           xatt_norm_g, mem_norm_g, xatt_wq, xatt_wkv, xatt_q_norm_g, xatt_k_norm_g, xatt_wo,
           mlp_norm_g, mlp_w1, mlp_w2):
    weights = (mix_norm_g, w_in, conv_w, conv_b, ssd_A_log, ssd_dt_bias, ssd_D, ssd_norm_g,
               att_q_norm_g, att_k_norm_g, att_out_norm_g, w_out, xatt_norm_g, mem_norm_g, xatt_wq,
               xatt_wkv, xatt_q_norm_g, xatt_k_norm_g, xatt_wo, mlp_norm_g, mlp_w1, mlp_w2)
    assert all(w.shape[0] == 1 for w in weights), "single-layer stack expected"
    prm = _prepare((x_prompt.shape[1], x_sample.shape[1]), *(w[0] for w in weights))
    return (_layer(x_prompt, mem_prompt, prm), _layer(x_sample, mem_sample, prm))
```

```python
import functools
import math

import jax
import jax.numpy as jnp
from jax import lax
from jax.experimental import pallas as pl
from jax.experimental.pallas import tpu as pltpu

F32 = jnp.float32
BF16 = jnp.bfloat16

D_MODEL = 1024
HEAD_DIM = 64
SSD_HEADS = 8
SSD_WIDTH = SSD_HEADS * HEAD_DIM
SSD_GROUPS = 2
SSD_STATE = 128
SSD_CONV = 5
SSD_CHUNK = 128
ATT_HEADS = 8
ATT_WIDTH = ATT_HEADS * HEAD_DIM
DILATED_PATTERNS = ((128, 1), (512, 4), (2048, 16))
ROPE_DIMS = HEAD_DIM // 4
ROPE_THETA = 500000.0
MEM_LEN = 256
XATT_HEADS = 4
XATT_HEAD_DIM = D_MODEL // XATT_HEADS
D_FF = 4 * D_MODEL
EPS = 1e-6
NEG_INF = -1e30
CONV_CH = SSD_WIDTH + 2 * SSD_GROUPS * SSD_STATE
OFF_Z = 0
OFF_XBC = OFF_Z + SSD_WIDTH
OFF_DT = OFF_XBC + CONV_CH
OFF_Q = OFF_DT + 2 * SSD_HEADS
OFF_K = OFF_Q + ATT_WIDTH
OFF_V = OFF_K + ATT_WIDTH
IN_COLS = OFF_V + ATT_WIDTH

LANES = 128
BF16_ROWS = 16
VMEM_LIMIT = 56 * 1024 * 1024
SSD_CPS = 8
BAND_HALF = 64
ATT_Q = 128
ATT_KEYS = ATT_Q + 2 * BAND_HALF
ATT_BLK = 2048
ATT_GROUP = 16
ATT_PAIRS = 2
LOG2E = math.log2(math.e)


def _dot(a, b):
    return jnp.dot(a, b, preferred_element_type=F32)


def _dot_nt(a, b):
    return lax.dot_general(a, b, (((1,), (1,)), ((), ())), preferred_element_type=F32)


def _const_spec(shape):
    zeros = (0,) * len(shape)
    return pl.BlockSpec(shape, lambda *_: zeros, pipeline_mode=pl.Buffered(1))


def _params(sem):
    return pltpu.CompilerParams(dimension_semantics=sem, vmem_limit_bytes=VMEM_LIMIT)


def _lane(shape):
    return lax.broadcasted_iota(jnp.int32, shape, len(shape) - 1)


def _rms(x, g):
    ms = jnp.mean(x * x, axis=-1, keepdims=True)
    return x * lax.rsqrt(ms + EPS) * g


def _silu(x):
    h = 0.5 * x
    return h + h * jnp.tanh(h)


def _emit_layouts(t, nat_out, c4_out, c16_out, slab_ref, slab2_ref):
    tm = t.shape[0]
    n4, n16 = tm // 4, tm // 16
    for p in range(ATT_WIDTH // LANES):
        tp = t[:, p * LANES:(p + 1) * LANES]
        nat_out[p] = tp.astype(BF16)
        slab_ref[p] = tp
        for r4 in range(4):
            c4 = slab_ref[p, pl.ds(r4, n4, stride=4), :]
            c4_out[0, p, r4] = c4.astype(BF16)
            slab2_ref[p, r4 * n4:(r4 + 1) * n4, :] = c4
        for r4 in range(4):
            for a in range(4):
                c16 = slab2_ref[p, pl.ds(r4 * n4 + a, n16, stride=4), :]
                c16_out[0, p, 4 * a + r4] = c16.astype(BF16)


def _inproj_kernel(x_ref, g_ref, wz_ref, wxbc_ref, wdt_ref, wq_ref, wk_ref, wv_ref, ones_ref,
                   qg_ref, kg_ref, rope_ref,
                   xz_out, dt_out, q1_out, q4_out, q16_out, kv1_out, kv4_out, kv16_out,
                   slab_ref, slab2_ref):
    xn = _rms(x_ref[...], g_ref[...]).astype(BF16)

    cos = jnp.concatenate([rope_ref[:, 0:LANES]] * 4, axis=1)
    sdn = jnp.concatenate([rope_ref[:, LANES:2 * LANES]] * 4, axis=1)
    sup = jnp.concatenate([rope_ref[:, 2 * LANES:]] * 4, axis=1)

    def norm_rope(t, gain_ref, scale):
        ssq = _dot((t * t).astype(BF16), ones_ref[...])
        t = t * lax.rsqrt(ssq * (1.0 / HEAD_DIM) + EPS) * gain_ref[...]
        half = ROPE_DIMS // 2
        t = (t * cos + pltpu.roll(t, half, 1) * sdn
             + pltpu.roll(t, ATT_WIDTH - half, 1) * sup)
        return t * scale

    tq = _dot(xn, wq_ref[...])
    xz_out[:, CONV_CH:] = _dot(xn, wz_ref[...]).astype(BF16)
    _emit_layouts(norm_rope(tq, qg_ref, LOG2E / math.sqrt(HEAD_DIM)), q1_out, q4_out, q16_out,
                  slab_ref.at[0], slab2_ref.at[0])
    tk = _dot(xn, wk_ref[...])
    xz_out[:, :CONV_CH] = _dot(xn, wxbc_ref[...]).astype(BF16)
    _emit_layouts(norm_rope(tk, kg_ref, 1.0), kv1_out.at[0], kv4_out.at[0], kv16_out.at[0],
                  slab_ref.at[1], slab2_ref.at[1])
    tv = _dot(xn, wv_ref[...])
    dt_out[...] = _dot(xn, wdt_ref[...])
    _emit_layouts(tv, kv1_out.at[1], kv4_out.at[1], kv16_out.at[1], slab_ref.at[2], slab2_ref.at[2])


def _inproj(x2, bsz, seq, prm, tm):
    t_tokens = x2.shape[0]
    tiles_per_seq = seq // tm
    tok = lambda w: pl.BlockSpec((tm, w), lambda i: (i, 0))
    rope = pl.BlockSpec((tm, 3 * LANES), lambda i: (i % tiles_per_seq, 0))
    npair = ATT_WIDTH // LANES
    cls = lambda d: pl.BlockSpec((1, npair, d, tm // d, LANES),
                                 lambda i: (i // tiles_per_seq, 0, 0, i % tiles_per_seq, 0))
    cls2 = lambda d: pl.BlockSpec((2, 1, npair, d, tm // d, LANES),
                                  lambda i: (0, i // tiles_per_seq, 0, 0, i % tiles_per_seq, 0))
    cls_shape = lambda d: (bsz, npair, d, seq // d, LANES)
    out_shape = (
        jax.ShapeDtypeStruct((t_tokens, CONV_CH + SSD_WIDTH), BF16),
        jax.ShapeDtypeStruct((t_tokens, LANES), F32),
        jax.ShapeDtypeStruct((npair, t_tokens, LANES), BF16),
        jax.ShapeDtypeStruct(cls_shape(4), BF16),
        jax.ShapeDtypeStruct(cls_shape(16), BF16),
        jax.ShapeDtypeStruct((2, npair, t_tokens, LANES), BF16),
        jax.ShapeDtypeStruct((2,) + cls_shape(4), BF16),
        jax.ShapeDtypeStruct((2,) + cls_shape(16), BF16),
    )
    out_specs = (tok(CONV_CH + SSD_WIDTH), tok(LANES),
                 pl.BlockSpec((npair, tm, LANES), lambda i: (0, i, 0)), cls(4), cls(16),
                 pl.BlockSpec((2, npair, tm, LANES), lambda i: (0, 0, i, 0)), cls2(4), cls2(16))
    return pl.pallas_call(
        _inproj_kernel,
        grid=(t_tokens // tm,),
        in_specs=[
            tok(D_MODEL), _const_spec((1, D_MODEL)),
            _const_spec((D_MODEL, SSD_WIDTH)), _const_spec((D_MODEL, CONV_CH)),
            _const_spec((D_MODEL, LANES)), _const_spec((D_MODEL, ATT_WIDTH)),
            _const_spec((D_MODEL, ATT_WIDTH)), _const_spec((D_MODEL, ATT_WIDTH)),
            _const_spec((ATT_WIDTH, ATT_WIDTH)),
            _const_spec((1, ATT_WIDTH)), _const_spec((1, ATT_WIDTH)),
            rope,
        ],
        out_specs=out_specs,
        out_shape=out_shape,
        scratch_shapes=[pltpu.VMEM((3, ATT_WIDTH // LANES, tm, LANES), F32),
                        pltpu.VMEM((3, ATT_WIDTH // LANES, tm, LANES), F32)],
        compiler_params=_params(("parallel",)),
        name="inproj",
    )(x2, prm["mix_g"], prm["w_z"], prm["w_xbc"], prm["w_dt"], prm["w_q"], prm["w_k"], prm["w_v"],
      prm["head_ones"], prm["qg"], prm["kg"], prm["rope"][seq])


def _split3(a):
    a1 = a.astype(BF16)
    r1 = a - a1.astype(F32)
    a2 = r1.astype(BF16)
    a3 = (r1 - a2.astype(F32)).astype(BF16)
    return a1, a2, a3


def _dot_split(lhs_bf16, a):
    a1, a2, a3 = _split3(a)
    return _dot(lhs_bf16, a1) + _dot(lhs_bf16, a2) + _dot(lhs_bf16, a3)


def _ssd_kernel(nsteps, xbc_ref, xprev_ref, xnext_ref, z_ref, dt_ref, convw_ref, convb_ref, alog_ref,
                dtbias_ref, drow_ref, ng_ref, out_ref, ext_ref, carry_ref, prevb_ref, conv_ref):
    q = SSD_CHUNK
    nh = SSD_HEADS
    width = SSD_WIDTH // SSD_GROUPS
    pad = (SSD_CONV - 1) // 2
    s = pl.program_id(1)
    backward = s < nsteps
    step = jnp.where(backward, nsteps - 1 - s, s - nsteps)

    lane = _lane((q, LANES))
    row = lax.broadcasted_iota(jnp.int32, (q, LANES), 0)
    erow = lax.broadcasted_iota(jnp.int32, (LANES, 2 * SSD_WIDTH), 0)
    ecol = lax.broadcasted_iota(jnp.int32, (LANES, 2 * SSD_WIDTH), 1)
    expand = (erow == ecol // HEAD_DIM).astype(BF16)
    ltri = (lane <= row).astype(BF16)
    utri = (lane >= row).astype(BF16)

    lane_c = _lane((q, SSD_CPS * LANES)) % LANES

    def side_by_side(t):
        return jnp.concatenate([t[j * q:(j + 1) * q] for j in range(SSD_CPS)], axis=1)

    def stacked(t):
        return jnp.concatenate([t[:, j * LANES:(j + 1) * LANES] for j in range(SSD_CPS)], axis=0)

    def expand_rows(v, lo, hi):
        return _dot(v.astype(BF16), expand[:, lo:hi])

    def expand_rows_exact(v, lo, hi):
        v1, v2, v3 = _split3(v)
        e = expand[:, lo:hi]
        return _dot(v1, e) + _dot(v2, e) + _dot(v3, e)

    def step_decays():
        raw = dt_ref[0] + dtbias_ref[...]
        softplus = jnp.maximum(raw, 0.0) + jnp.log(1.0 + jnp.exp(-jnp.abs(raw)))
        dt = jnp.where(_lane(raw.shape) < 2 * nh, softplus, 0.0)
        return side_by_side(dt), side_by_side(dt * (-jnp.exp(alog_ref[...])))

    def chunk_states(xs, bm16, w_full):
        xw = (xs * w_full).astype(BF16)
        parts = []
        for g in range(SSD_GROUPS):
            bt = bm16[:, g * SSD_STATE:(g + 1) * SSD_STATE].astype(F32).T.astype(BF16)
            parts.append(_dot(bt, xw[:, g * width:(g + 1) * width]))
        return jnp.concatenate(parts, axis=1)

    @pl.when(s == 0)
    def _():
        carry_ref[...] = jnp.zeros_like(carry_ref)

    @pl.when(s == nsteps)
    def _():
        carry_ref[...] = jnp.zeros_like(carry_ref)

    @pl.when(backward)
    def _():
        rows = SSD_CPS * q
        prev = xprev_ref[0]
        nxt = xnext_ref[0]
        ext_ref[0:BF16_ROWS, :] = jnp.where(step > 0, prev, jnp.zeros_like(prev))
        ext_ref[BF16_ROWS:BF16_ROWS + rows, :] = xbc_ref[0]
        ext_ref[BF16_ROWS + rows:, :] = jnp.where(step < nsteps - 1, nxt, jnp.zeros_like(nxt))

        win_rows = q + 2 * BF16_ROWS
        srow = lax.broadcasted_iota(jnp.int32, (q, win_rows), 0)
        scol = lax.broadcasted_iota(jnp.int32, (q, win_rows), 1)
        shifts = {k: (scol == srow + BF16_ROWS + k - pad).astype(BF16)
                  for k in range(SSD_CONV) if k != pad}

        bmask = (lane_c >= nh) & (lane_c < 2 * nh)
        dt, a = step_decays()
        sb = _dot_split(utri, a)
        wb = jnp.where(bmask, jnp.exp(sb[0:1, :] - sb) * dt, 0.0)
        w_full = expand_rows(stacked(wb), SSD_WIDTH, 2 * SSD_WIDTH)
        decay = expand_rows_exact(stacked(jnp.where(bmask[0:1], jnp.exp(sb[0:1, :]), 0.0)),
                                  SSD_WIDTH, 2 * SSD_WIDTH)

        def conv_dots(j):
            win = ext_ref[j * q:j * q + win_rows, :]
            return win, {k: _dot(sh, win) for k, sh in shifts.items()}

        def conv_finish(j, win, dots):
            acc = convb_ref[...] + win[BF16_ROWS:BF16_ROWS + q].astype(F32) * convw_ref[pad:pad + 1, :]
            for k, d in dots.items():
                acc = acc + d * convw_ref[k:k + 1, :]
            conv_ref[step * SSD_CPS + j] = _silu(acc).astype(BF16)

        pending = conv_dots(0)
        for j in range(SSD_CPS):
            ahead = conv_dots(j + 1) if j + 1 < SSD_CPS else None
            conv_finish(j, *pending)
            pending = ahead

        states = []
        for j in range(SSD_CPS):
            xb16 = conv_ref[step * SSD_CPS + j]
            states.append(chunk_states(xb16[:, :SSD_WIDTH].astype(F32),
                                       xb16[:, SSD_WIDTH:SSD_WIDTH + SSD_GROUPS * SSD_STATE],
                                       w_full[j * q:(j + 1) * q]))
        carry = carry_ref[...]
        for j in reversed(range(SSD_CPS)):
            prevb_ref[step * SSD_CPS + j] = carry.astype(BF16)
            carry = carry * decay[j:j + 1] + states[j]
        carry_ref[...] = carry

    @pl.when(jnp.logical_not(backward))
    def _():
        ii = lax.broadcasted_iota(jnp.int32, (q, q), 0)
        jj = lax.broadcasted_iota(jnp.int32, (q, q), 1)
        lane_p = _lane((q, LANES))
        fmask = lane_c < nh
        both = lane_c < 2 * nh

        dt, a = step_decays()
        sums = _dot_split(jnp.concatenate([ltri, utri], axis=0), a)
        cf, sb = sums[:q], sums[q:]
        ldt = jnp.where(both, jnp.log2(dt), 0.0)
        pack_all = jnp.where(fmask, cf * LOG2E,
                             jnp.where(both, sb * LOG2E, pltpu.roll(ldt, 2 * nh, 1)))
        pack_t_all = pack_all.T
        dsum = dt + pltpu.roll(dt, SSD_CPS * LANES - nh, 1)
        dsum_t_all = jnp.where(fmask, jnp.log2(dsum), 0.0).T
        edge_all = expand_rows(stacked(jnp.where(both, jnp.exp2(pack_all), 0.0)), 0, 2 * SSD_WIDTH)
        last = cf[q - 1:q, :]
        w_full = expand_rows(stacked(jnp.where(fmask, jnp.exp(last - cf) * dt, 0.0)), 0, SSD_WIDTH)
        decay = expand_rows_exact(stacked(jnp.where(fmask[0:1], jnp.exp(last), 0.0)), 0, SSD_WIDTH)

        states = []
        for j in range(SSD_CPS):
            xb16 = conv_ref[step * SSD_CPS + j]
            states.append(chunk_states(xb16[:, :SSD_WIDTH].astype(F32),
                                       xb16[:, SSD_WIDTH:SSD_WIDTH + SSD_GROUPS * SSD_STATE],
                                       w_full[j * q:(j + 1) * q]))
        prevs = []
        carry = carry_ref[...]
        for j in range(SSD_CPS):
            prevs.append(carry.astype(BF16))
            carry = carry * decay[j:j + 1] + states[j]
        carry_ref[...] = carry

        def decay_mats(j):
            pack = pack_all[:, j * LANES:(j + 1) * LANES]
            pack_t = pack_t_all[j * LANES:(j + 1) * LANES]
            dsum_t = dsum_t_all[j * LANES:(j + 1) * LANES]
            xb16 = conv_ref[step * SSD_CPS + j]
            bm16 = xb16[:, SSD_WIDTH:SSD_WIDTH + SSD_GROUPS * SSD_STATE]
            cm16 = xb16[:, SSD_WIDTH + SSD_GROUPS * SSD_STATE:]
            cb = [_dot_nt(cm16[:, g * SSD_STATE:(g + 1) * SSD_STATE],
                          bm16[:, g * SSD_STATE:(g + 1) * SSD_STATE]) for g in range(SSD_GROUPS)]
            mats = []
            for h in range(nh):
                low_row = pack_t[h:h + 1, :] - pack_t[2 * nh + h:2 * nh + h + 1, :]
                up_row = pack_t[nh + h:nh + h + 1, :] - pack_t[3 * nh + h:3 * nh + h + 1, :]
                expo = jnp.where(jj < ii, pack[:, h:h + 1] - low_row,
                                 jnp.where(jj > ii, pack[:, nh + h:nh + h + 1] - up_row,
                                           dsum_t[h:h + 1, :]))
                mats.append((cb[h // (nh // SSD_GROUPS)] * jnp.exp2(expo)).astype(BF16))
            return mats

        def finish(j, mats):
            c = step * SSD_CPS + j
            edge = edge_all[j * q:(j + 1) * q]
            xb16 = conv_ref[c]
            xs16 = xb16[:, :SSD_WIDTH]
            xs = xs16.astype(F32)
            cm16 = xb16[:, SSD_WIDTH + SSD_GROUPS * SSD_STATE:]
            y_parts = []
            for p in range(nh // 2):
                xp = xs16[:, p * LANES:(p + 1) * LANES]
                zero = jnp.zeros_like(xp)
                rhs = jnp.concatenate([jnp.where(lane_p < HEAD_DIM, xp, zero),
                                       jnp.where(lane_p >= HEAD_DIM, xp, zero)], axis=0)
                lhs = jnp.concatenate([mats[2 * p], mats[2 * p + 1]], axis=1)
                y_parts.append(_dot(lhs, rhs))
            y = jnp.concatenate(y_parts, axis=1)

            prev_f = prevs[j]
            prev_b = prevb_ref[c]
            off_parts = []
            for g in range(SSD_GROUPS):
                st = jnp.concatenate([prev_f[:, g * width:(g + 1) * width],
                                      prev_b[:, g * width:(g + 1) * width]], axis=1)
                res = _dot(cm16[:, g * SSD_STATE:(g + 1) * SSD_STATE], st)
                off_parts.append(res[:, :width] * edge[:, g * width:(g + 1) * width]
                                 + res[:, width:] * edge[:, SSD_WIDTH + g * width:
                                                         SSD_WIDTH + (g + 1) * width])
            y = y + jnp.concatenate(off_parts, axis=1) + drow_ref[...] * xs

            y = y * _silu(z_ref[0, j * q:(j + 1) * q, :].astype(F32))
            out_ref[0, j * q:(j + 1) * q, :] = _rms(y, ng_ref[...]).astype(BF16)

        for j in range(SSD_CPS):
            finish(j, decay_mats(j))


def _ssd(xz, dt, prm, bsz, seq):
    q = SSD_CHUNK
    rows = SSD_CPS * q
    assert seq % rows == 0
    nsteps = seq // rows
    nc = seq // q
    halo_blocks = rows // BF16_ROWS
    conv_step = lambda s: jnp.where(s < nsteps, nsteps - 1 - s, 0)
    both_step = lambda s: jnp.where(s < nsteps, nsteps - 1 - s, s - nsteps)
    fwd_step = lambda s: jnp.maximum(s - nsteps, 0)
    xz3 = xz.reshape(bsz, seq, CONV_CH + SSD_WIDTH)
    gate_blk = CONV_CH // SSD_WIDTH
    dt3 = dt.reshape(bsz, seq, LANES)
    in_specs = [
        pl.BlockSpec((1, rows, CONV_CH), lambda b, s: (b, conv_step(s), 0)),
        pl.BlockSpec((1, BF16_ROWS, CONV_CH),
                     lambda b, s: (b, jnp.maximum(conv_step(s) * halo_blocks - 1, 0), 0)),
        pl.BlockSpec((1, BF16_ROWS, CONV_CH),
                     lambda b, s: (b, jnp.minimum((conv_step(s) + 1) * halo_blocks,
                                                  seq // BF16_ROWS - 1), 0)),
        pl.BlockSpec((1, rows, SSD_WIDTH), lambda b, s: (b, fwd_step(s), gate_blk)),
        pl.BlockSpec((1, rows, LANES), lambda b, s: (b, both_step(s), 0)),
        _const_spec((SSD_CONV, CONV_CH)), _const_spec((1, CONV_CH)),
        _const_spec((1, LANES)), _const_spec((1, LANES)),
        _const_spec((1, SSD_WIDTH)), _const_spec((1, SSD_WIDTH)),
    ]
    out_spec = pl.BlockSpec((1, rows, SSD_WIDTH), lambda b, s: (b, fwd_step(s), 0))
    out = pl.pallas_call(
        functools.partial(_ssd_kernel, nsteps),
        grid=(bsz, 2 * nsteps),
        in_specs=in_specs,
        out_specs=out_spec,
        out_shape=jax.ShapeDtypeStruct((bsz, seq, SSD_WIDTH), BF16),
        scratch_shapes=[
            pltpu.VMEM((rows + 2 * BF16_ROWS, CONV_CH), BF16),
            pltpu.VMEM((SSD_STATE, SSD_WIDTH), F32),
            pltpu.VMEM((nc, SSD_STATE, SSD_WIDTH), BF16),
            pltpu.VMEM((nc, q, CONV_CH), BF16),
        ],
        compiler_params=_params(("parallel", "arbitrary")),
        name="ssd",
    )(xz3, xz3, xz3, xz3, dt3, prm["conv_w"], prm["conv_b"], prm["alog_row"], prm["dtbias_row"],
      prm["d_row"], prm["ssd_ng"])
    return out.reshape(bsz * seq, SSD_WIDTH)


def _attn_unit(q_tile, kwin_ref, vwin_ref, row0, bias, m_old):
    low = _lane((ATT_Q, LANES)) < HEAD_DIM
    zero = jnp.zeros_like(q_tile)
    lhs = jnp.concatenate([jnp.where(low, q_tile, zero), jnp.where(low, zero, q_tile)], axis=0)
    sc = _dot_nt(lhs, kwin_ref[pl.ds(row0, ATT_KEYS), :]) + bias
    part = jnp.maximum(sc[:, :LANES], sc[:, LANES:])
    if m_old is not None:
        ninf = jnp.full_like(m_old, NEG_INF)
        part = jnp.maximum(part, jnp.concatenate([jnp.where(low, m_old, ninf),
                                                  jnp.where(low, ninf, m_old)], axis=0))
    mx = jnp.max(part, axis=-1, keepdims=True)
    pr = jnp.exp2(sc - mx).astype(BF16)
    ones = jnp.ones((ATT_KEYS, LANES), BF16)
    vcat = jnp.concatenate([vwin_ref[pl.ds(row0, ATT_KEYS), :], ones], axis=1)
    res = _dot(pr, vcat)
    num = jnp.where(low, res[:ATT_Q, :LANES], res[ATT_Q:, :LANES])
    den = jnp.where(low, res[:ATT_Q, LANES:], res[ATT_Q:, LANES:])
    mxp = jnp.where(low, jnp.broadcast_to(mx[:ATT_Q], (ATT_Q, LANES)),
                    jnp.broadcast_to(mx[ATT_Q:], (ATT_Q, LANES)))
    return num, den, mxp


def _attn_pair(nblk, n, q1, k1c, k1p, k1n, v1c, v1p, v1n, q4, k4c, k4p, k4n, v4c, v4p, v4n,
               q16, k16c, k16p, k16n, v16c, v16p, v16n, o_out, kwin, vwin, acc_ref, den_ref, max_ref,
               bias_ref):
    half = BAND_HALF

    def bias_for(first_sub, last_sub):
        idx = jnp.where((n == 0) & first_sub, 1, jnp.where((n == nblk - 1) & last_sub, 2, 0))
        return bias_ref[idx]

    def fill(win, base, prev, cur, nxt, rows):
        win[pl.ds(base, half), :] = prev
        win[pl.ds(base + half, rows), :] = cur
        win[pl.ds(base + half + rows, half), :] = nxt

    def load_state(rows):
        return acc_ref[rows, :], den_ref[rows, :], max_ref[rows, :]

    def merge(old, num, den, mxp):
        scale = jnp.exp2(old[2] - mxp)
        return old[0] * scale + num, old[1] * scale + den, mxp

    def store_state(rows, new):
        acc_ref[rows, :] = new[0]
        den_ref[rows, :] = new[1]
        max_ref[rows, :] = new[2]

    span16 = ATT_KEYS

    def fill16(r, carry):
        base = pl.multiple_of(r * span16, span16)
        fill(kwin, base, k16p[0, r], k16c[0, r], k16n[0, r], ATT_Q)
        fill(vwin, base, v16p[0, r], v16c[0, r], v16n[0, r], ATT_Q)
        return carry

    lax.fori_loop(0, 16, fill16, 0)

    def group16(g, carry):
        new = []
        for u in range(ATT_GROUP):
            r = g * ATT_GROUP + u
            new.append(_attn_unit(q16[0, r], kwin, vwin, pl.multiple_of(r * span16, span16),
                                  bias_for(True, True), None))
        for u in range(ATT_GROUP):
            store_state(pl.ds(g * ATT_GROUP + u, ATT_Q, stride=16), new[u])
        return carry

    lax.fori_loop(0, 16 // ATT_GROUP, group16, 0)

    rows4 = ATT_BLK // 4
    nsub4 = rows4 // ATT_Q
    span4 = rows4 + 2 * half
    for r in range(4):
        fill(kwin, r * span4, k4p[0, r], k4c[0, r], k4n[0, r], rows4)
        fill(vwin, r * span4, v4p[0, r], v4c[0, r], v4n[0, r], rows4)

    ncls = ATT_GROUP // nsub4

    def group4(g, carry):
        units = [(g * ncls + u, j) for u in range(ncls) for j in range(nsub4)]
        rows = [pl.ds(r + 4 * ATT_Q * j, ATT_Q, stride=4) for r, j in units]
        old = [load_state(rw) for rw in rows]
        new = []
        for (r, j), prev in zip(units, old):
            unit = _attn_unit(q4[0, r, j * ATT_Q:(j + 1) * ATT_Q, :], kwin, vwin,
                              pl.multiple_of(r * span4, ATT_Q) + j * ATT_Q,
                              bias_for(j == 0, j == nsub4 - 1), prev[2])
            new.append(merge(prev, *unit))
        for rw, nw in zip(rows, new):
            store_state(rw, nw)
        return carry

    lax.fori_loop(0, 4 // ncls, group4, 0)

    fill(kwin, 0, k1p[0], k1c[0], k1n[0], ATT_BLK)
    fill(vwin, 0, v1p[0], v1c[0], v1n[0], ATT_BLK)
    nsub1 = ATT_BLK // ATT_Q

    def group1(g, carry):
        for u in range(ATT_GROUP):
            j = g * ATT_GROUP + u
            r0 = pl.multiple_of(j * ATT_Q, ATT_Q)
            rows = pl.ds(r0, ATT_Q)
            old = load_state(rows)
            unit = _attn_unit(q1[0, rows, :], kwin, vwin, r0, bias_for(j == 0, j == nsub1 - 1),
                              old[2])
            acc_new, den_new, _ = merge(old, *unit)
            o_out[0, rows, :] = (acc_new / den_new).astype(BF16)
        return carry

    lax.fori_loop(0, nsub1 // ATT_GROUP, group1, 0)


def _attn_kernel(nblk, q1, kv1c, kv1p, kv1n, q4, kv4c, kv4p, kv4n, q16, kv16c, kv16p, kv16n,
                 o_out, kwin, vwin, acc_ref, den_ref, max_ref, bias_ref):
    n = pl.program_id(2)
    half = BAND_HALF

    @pl.when((pl.program_id(0) == 0) & (pl.program_id(1) == 0) & (n == 0))
    def _():
        qi = lax.broadcasted_iota(jnp.int32, (2 * ATT_Q, ATT_KEYS), 0) % ATT_Q
        kt = lax.broadcasted_iota(jnp.int32, (2 * ATT_Q, ATT_KEYS), 1)
        band = jnp.where(jnp.abs(kt - half - qi) <= half, 0.0, NEG_INF)
        bias_ref[0] = band
        bias_ref[1] = jnp.where(kt < half, NEG_INF, band)
        bias_ref[2] = jnp.where(kt >= half + ATT_Q, NEG_INF, band)

    for pp in range(ATT_PAIRS):
        nat = lambda r: r.at[pp]
        cls = lambda r: r.at[0].at[pp:pp + 1]
        kv = lambda view, refs: [view(r.at[i]) for i in range(2) for r in refs]
        _attn_pair(nblk, n, nat(q1), *kv(nat, (kv1c, kv1p, kv1n)),
                   cls(q4), *kv(cls, (kv4c, kv4p, kv4n)),
                   cls(q16), *kv(cls, (kv16c, kv16p, kv16n)),
                   cls(o_out), kwin, vwin, acc_ref, den_ref, max_ref, bias_ref)


def _dilated_attention(qkv, bsz, seq):
    (q1, q4, q16), (kv1, kv4, kv16) = qkv
    assert tuple(d for _, d in DILATED_PATTERNS) == (1, 4, 16)
    assert all(w // (2 * d) == BAND_HALF for w, d in DILATED_PATTERNS)
    assert seq % ATT_BLK == 0 and seq // 16 >= 2 * ATT_Q
    nblk = seq // ATT_BLK
    npair = ATT_WIDTH // LANES
    def specs(dil):
        rows = ATT_BLK // dil
        per_blk = rows // BAND_HALF
        last = seq // dil // BAND_HALF - 1
        before = lambda n: jnp.maximum(n * per_blk - 1, 0)
        after = lambda n: jnp.minimum((n + 1) * per_blk, last)
        if dil == 1:
            blk = lambda r: (ATT_PAIRS, 1, r, LANES)
            idx = lambda b, p, m: (p, b, m, 0)
        else:
            blk = lambda r: (1, ATT_PAIRS, dil, r, LANES)
            idx = lambda b, p, m: (b, p, 0, m, 0)
        return [
            pl.BlockSpec(blk(rows), lambda b, p, n: idx(b, p, n)),
            pl.BlockSpec((2,) + blk(rows), lambda b, p, n: (0,) + idx(b, p, n)),
            pl.BlockSpec((2,) + blk(BAND_HALF), lambda b, p, n: (0,) + idx(b, p, before(n))),
            pl.BlockSpec((2,) + blk(BAND_HALF), lambda b, p, n: (0,) + idx(b, p, after(n))),
        ]

    q1 = q1.reshape(npair, bsz, seq, LANES)
    kv1 = kv1.reshape(2, npair, bsz, seq, LANES)
    args = [q1] + [kv1] * 3 + [q4] + [kv4] * 3 + [q16] + [kv16] * 3
    win_rows = 16 * ATT_KEYS
    out = pl.pallas_call(
        functools.partial(_attn_kernel, nblk),
        grid=(bsz, npair // ATT_PAIRS, nblk),
        in_specs=specs(1) + specs(4) + specs(16),
        out_specs=pl.BlockSpec((1, ATT_PAIRS, ATT_BLK, LANES), lambda b, p, n: (b, p, n, 0)),
        out_shape=jax.ShapeDtypeStruct((bsz, npair, seq, LANES), BF16),
        scratch_shapes=[pltpu.VMEM((win_rows, LANES), BF16),
                        pltpu.VMEM((win_rows, LANES), BF16),
                        pltpu.VMEM((ATT_BLK, LANES), F32),
                        pltpu.VMEM((ATT_BLK, LANES), F32),
                        pltpu.VMEM((ATT_BLK, LANES), F32),
                        pltpu.VMEM((3, 2 * ATT_Q, ATT_KEYS), F32)],
        compiler_params=_params(("arbitrary", "arbitrary", "arbitrary")),
        name="attn",
    )(*args)
    return out


def _memkv_kernel(mem_ref, g_ref, wkv_ref, kg_ref, kt_out, v_out):
    mn = _rms(mem_ref[0], g_ref[...]).astype(BF16)
    kv = _dot(mn, wkv_ref[...])
    scale = 1.0 / math.sqrt(XATT_HEAD_DIM)
    for h in range(XATT_HEADS):
        kh = kv[:, h * XATT_HEAD_DIM:(h + 1) * XATT_HEAD_DIM]
        kh = _rms(kh, kg_ref[...]) * scale
        kt_out[0, h * XATT_HEAD_DIM:(h + 1) * XATT_HEAD_DIM, :] = kh.T.astype(BF16)
    v_out[0] = kv[:, D_MODEL:].astype(BF16)


def _memkv(mem, prm):
    bsz = mem.shape[0]
    return pl.pallas_call(
        _memkv_kernel,
        grid=(bsz,),
        in_specs=[pl.BlockSpec((1, MEM_LEN, D_MODEL), lambda b: (b, 0, 0)),
                  _const_spec((1, D_MODEL)), _const_spec((D_MODEL, 2 * D_MODEL)),
                  _const_spec((1, XATT_HEAD_DIM))],
        out_specs=(pl.BlockSpec((1, D_MODEL, MEM_LEN), lambda b: (b, 0, 0)),
                   pl.BlockSpec((1, MEM_LEN, D_MODEL), lambda b: (b, 0, 0))),
        out_shape=(jax.ShapeDtypeStruct((bsz, D_MODEL, MEM_LEN), BF16),
                   jax.ShapeDtypeStruct((bsz, MEM_LEN, D_MODEL), BF16)),
        compiler_params=_params(("parallel",)),
        name="memkv",
    )(mem, prm["mem_g"], prm["w_kv"], prm["xkg"])


def _post_kernel(x_ref, ssd_ref, att_ref, og_ref, wo1_ref, wo2_ref, g_ref, wq_ref, qg_ref, kt_ref,
                 v_ref, wo_ref, h_out):
    tm = x_ref.shape[0]
    halves = [slice(i * tm // 2, (i + 1) * tm // 2) for i in range(2)]

    def mix(rows):
        att = jnp.concatenate([att_ref[0, p, rows, :] for p in range(ATT_WIDTH // LANES)], axis=1)
        att = _rms(att.astype(F32), og_ref[...]).astype(BF16)
        return x_ref[rows, :] + _dot(ssd_ref[rows, :], wo1_ref[...]) + _dot(att, wo2_ref[...])

    def query(h):
        return _dot(_rms(h, g_ref[...]).astype(BF16), wq_ref[...])

    def head(hd):
        return slice(hd * XATT_HEAD_DIM, (hd + 1) * XATT_HEAD_DIM)

    def scores(qf, hd):
        return _dot(_rms(qf[:, head(hd)], qg_ref[...]).astype(BF16), kt_ref[0, head(hd), :])

    def weighted(sc, hd):
        mx = jnp.max(sc, axis=-1, keepdims=True)
        pr = jnp.exp(sc - mx)
        den = jnp.sum(pr, axis=-1, keepdims=True)
        return _dot(pr.astype(BF16), v_ref[0, :, head(hd)]) / den

    hs = [mix(rows) for rows in halves]
    qs = [query(h) for h in hs]
    parts = [[], []]
    for hd in range(XATT_HEADS):
        scs = [scores(qf, hd) for qf in qs]
        for i, sc in enumerate(scs):
            parts[i].append(weighted(sc, hd))
    os_ = [jnp.concatenate(p, axis=1).astype(BF16) for p in parts]
    for rows, h, o in zip(halves, hs, os_):
        h_out[rows, :] = h + _dot(o, wo_ref[...])


def _post(x2, ssd, att, kt, vx, prm, seq, tm):
    t_tokens = x2.shape[0]
    tiles_per_seq = seq // tm
    tok = lambda w: pl.BlockSpec((tm, w), lambda i: (i, 0))
    half = D_MODEL // 2
    return pl.pallas_call(
        _post_kernel,
        grid=(t_tokens // tm,),
        in_specs=[tok(D_MODEL), tok(SSD_WIDTH),
                  pl.BlockSpec((1, ATT_WIDTH // LANES, tm, LANES),
                               lambda i: (i // tiles_per_seq, 0, i % tiles_per_seq, 0)),
                  _const_spec((1, ATT_WIDTH)),
                  _const_spec((half, D_MODEL)), _const_spec((half, D_MODEL)),
                  _const_spec((1, D_MODEL)), _const_spec((D_MODEL, D_MODEL)),
                  _const_spec((1, XATT_HEAD_DIM)),
                  pl.BlockSpec((1, D_MODEL, MEM_LEN), lambda i: (i // tiles_per_seq, 0, 0)),
                  pl.BlockSpec((1, MEM_LEN, D_MODEL), lambda i: (i // tiles_per_seq, 0, 0)),
                  _const_spec((D_MODEL, D_MODEL))],
        out_specs=tok(D_MODEL),
        out_shape=jax.ShapeDtypeStruct((t_tokens, D_MODEL), F32),
        compiler_params=_params(("parallel",)),
        name="post",
    )(x2, ssd, att, prm["att_og"], prm["w_out1"], prm["w_out2"], prm["xatt_g"], prm["w_xq"],
      prm["xqg"], kt, vx, prm["w_xo"])


def _mlp_kernel(ff_chunk, h_ref, g_ref, w1_ref, w2_ref, y_out):
    h = h_ref[...]
    hm = _rms(h, g_ref[...]).astype(BF16)
    acc = h
    for c in range(D_FF // ff_chunk):
        sl = slice(c * ff_chunk, (c + 1) * ff_chunk)
        a = jnp.maximum(_dot(hm, w1_ref[:, sl]), 0.0)
        acc = acc + _dot((a * a).astype(BF16), w2_ref[sl, :])
    y_out[...] = acc


def _mlp(h, prm, tm):
    t_tokens = h.shape[0]
    tok = pl.BlockSpec((tm, D_MODEL), lambda i: (i, 0))
    return pl.pallas_call(
        functools.partial(_mlp_kernel, 1024),
        grid=(t_tokens // tm,),
        in_specs=[tok, _const_spec((1, D_MODEL)), _const_spec((D_MODEL, D_FF)),
                  _const_spec((D_FF, D_MODEL))],
        out_specs=tok,
        out_shape=jax.ShapeDtypeStruct((t_tokens, D_MODEL), F32),
        compiler_params=_params(("parallel",)),
        name="mlp",
    )(h, prm["mlp_g"], prm["w1"], prm["w2"])


def _rope_tables(seq):
    half = ROPE_DIMS // 2
    dim = jnp.arange(LANES) % HEAD_DIM
    inv_freq = jnp.power(jnp.float32(ROPE_THETA), -(dim % half).astype(F32) / half)
    ang = jnp.arange(seq).astype(F32)[:, None] * inv_freq[None, :]
    sin = jnp.sin(ang)
    cos = jnp.where(dim < ROPE_DIMS, jnp.cos(ang), 1.0)
    sdn = jnp.where((dim >= half) & (dim < ROPE_DIMS), sin, 0.0)
    sup = jnp.where(dim < half, -sin, 0.0)
    return jnp.concatenate([cos, sdn, sup], axis=1)


def _prepare(seqs, mix_norm_g, w_in, conv_w, conv_b, ssd_A_log, ssd_dt_bias, ssd_D, ssd_norm_g,
             att_q_norm_g, att_k_norm_g, att_out_norm_g, w_out, xatt_norm_g, mem_norm_g, xatt_wq,
             xatt_wkv, xatt_q_norm_g, xatt_k_norm_g, xatt_wo, mlp_norm_g, mlp_w1, mlp_w2):
    row = lambda t: t.reshape(1, -1).astype(F32)
    pad_lanes = lambda t: jnp.pad(t, ((0, 0), (0, LANES - t.shape[1])))
    head_id = jnp.arange(ATT_WIDTH) // HEAD_DIM
    prm = {
        "mix_g": row(mix_norm_g),
        "w_z": w_in[:, OFF_Z:OFF_XBC].astype(BF16),
        "w_xbc": w_in[:, OFF_XBC:OFF_DT].astype(BF16),
        "w_dt": pad_lanes(w_in[:, OFF_DT:OFF_Q]).astype(BF16),
        "w_q": w_in[:, OFF_Q:OFF_K].astype(BF16),
        "w_k": w_in[:, OFF_K:OFF_V].astype(BF16),
        "w_v": w_in[:, OFF_V:IN_COLS].astype(BF16),
        "head_ones": (head_id[:, None] == head_id[None, :]).astype(BF16),
        "qg": jnp.tile(row(att_q_norm_g), (1, ATT_HEADS)),
        "kg": jnp.tile(row(att_k_norm_g), (1, ATT_HEADS)),
        "conv_w": conv_w.astype(F32),
        "conv_b": row(conv_b),
        "alog_row": pad_lanes(row(ssd_A_log)),
        "dtbias_row": pad_lanes(row(ssd_dt_bias)),
        "d_row": jnp.repeat(row(ssd_D), HEAD_DIM, axis=1),
        "ssd_ng": row(ssd_norm_g),
        "att_og": row(att_out_norm_g),
        "w_out1": w_out[:SSD_WIDTH].astype(BF16),
        "w_out2": w_out[SSD_WIDTH:].astype(BF16),
        "xatt_g": row(xatt_norm_g),
        "mem_g": row(mem_norm_g),
        "w_xq": xatt_wq.astype(BF16),
        "w_kv": xatt_wkv.astype(BF16),
        "xqg": row(xatt_q_norm_g),
        "xkg": row(xatt_k_norm_g),
        "w_xo": xatt_wo.astype(BF16),
        "mlp_g": row(mlp_norm_g),
        "w1": mlp_w1.astype(BF16),
        "w2": mlp_w2.astype(BF16),
    }
    table = _rope_tables(max(seqs))
    prm["rope"] = {seq: table for seq in set(seqs)}
    return prm


IN_TILE = 512
OUT_TILE = 1024


def _layer(x, mem, prm):
    bsz, seq, _ = x.shape
    x2 = x.reshape(bsz * seq, D_MODEL)
    outs = _inproj(x2, bsz, seq, prm, min(IN_TILE, seq))
    ssd = _ssd(outs[0], outs[1], prm, bsz, seq)
    att = _dilated_attention((outs[2:5], outs[5:8]), bsz, seq)
    kt, vx = _memkv(mem, prm)
    h = _post(x2, ssd, att, kt, vx, prm, seq, min(OUT_TILE, seq))
    y = _mlp(h, prm, min(OUT_TILE, seq))
    return y.reshape(bsz, seq, D_MODEL)


def kernel(x_prompt, x_sample, mem_prompt, mem_sample, mix_norm_g, w_in, conv_w, conv_b, ssd_A_log,
           ssd_dt_bias, ssd_D, ssd_norm_g, att_q_norm_g, att_k_norm_g, att_out_norm_g, w_out,
           xatt_norm_g, mem_norm_g, xatt_wq, xatt_wkv, xatt_q_norm_g, xatt_k_norm_g, xatt_wo,
           mlp_norm_g, mlp_w1, mlp_w2):
    weights = (mix_norm_g, w_in, conv_w, conv_b, ssd_A_log, ssd_dt_bias, ssd_D, ssd_norm_g,
               att_q_norm_g, att_k_norm_g, att_out_norm_g, w_out, xatt_norm_g, mem_norm_g, xatt_wq,
               xatt_wkv, xatt_q_norm_g, xatt_k_norm_g, xatt_wo, mlp_norm_g, mlp_w1, mlp_w2)
    assert all(w.shape[0] == 1 for w in weights), "single-layer stack expected"
    prm = _prepare((x_prompt.shape[1], x_sample.shape[1]), *(w[0] for w in weights))
    return (_layer(x_prompt, mem_prompt, prm), _layer(x_sample, mem_sample, prm))
```
